```python
import jax, jax.numpy as jnp
from jax import lax
import numpy as np

D_MODEL = 4096
BATCH = 2
SEQ = 8192
DEPTH = 1

CHUNK = 64
HEAD_DIM = 128
N_SB_HEADS = 16
N_FOX_HEADS = 16
SB_WIDTH = N_SB_HEADS * HEAD_DIM
FOX_WIDTH = N_FOX_HEADS * HEAD_DIM
MIX_WIDTH = SB_WIDTH + FOX_WIDTH
IN_PROJ_DIM = 3 * SB_WIDTH + 3 * FOX_WIDTH + N_FOX_HEADS
SPLIT_POINTS = (SB_WIDTH, 2 * SB_WIDTH, 3 * SB_WIDTH,
                3 * SB_WIDTH + FOX_WIDTH, 3 * SB_WIDTH + 2 * FOX_WIDTH,
                3 * SB_WIDTH + 3 * FOX_WIDTH)
D_FF = 11008
FFN_RESIDUAL_SCALE = 0.5
Q_BLOCK = 128
EPS = 1e-6

kernel_name = "hybrid_sb_fox_macaron_block"


def rms_norm(x, g):
    xf = x.astype(jnp.float32)
    y = xf * lax.rsqrt(jnp.mean(xf * xf, axis=-1, keepdims=True) + EPS)
    return (y * g.astype(jnp.float32)).astype(x.dtype)


def swiglu(x, w_gate, w_up, w_down):
    return jnp.einsum('bsf,fd->bsd',
                      jax.nn.silu(jnp.einsum('bsd,df->bsf', x, w_gate)) * jnp.einsum('bsd,df->bsf', x, w_up),
                      w_down)


def to_heads(t, n_heads):
    b, s, _ = t.shape
    return t.reshape(b, s, n_heads, HEAD_DIM).transpose(0, 2, 1, 3)


def head_rms_norm(o, g):
    b, h, s, d = o.shape
    of = o.transpose(0, 2, 1, 3).astype(jnp.float32)
    y = of * lax.rsqrt(jnp.mean(of * of, axis=-1, keepdims=True) + EPS)
    y = y * g.astype(jnp.float32).reshape(h, d)
    return y.reshape(b, s, h * d).astype(o.dtype)


def stick_breaking_attention(q, k, v):
    b, h, s, d = q.shape
    nb = s // Q_BLOCK
    scale = d ** -0.5
    kpos = jnp.arange(s)
    q_blocks = q.reshape(b, h, nb, Q_BLOCK, d).transpose(2, 0, 1, 3, 4)

    def block(args):
        q_blk, i = args
        qpos = i * Q_BLOCK + jnp.arange(Q_BLOCK)
        z = jnp.einsum('bhqd,bhkd->bhqk', q_blk, k).astype(jnp.float32) * scale
        past = kpos[None, :] < qpos[:, None]
        log_keep = jnp.where(past, jax.nn.log_sigmoid(-z), 0.0)
        after = lax.cumsum(log_keep, axis=3, reverse=True) - log_keep
        a = jnp.where(past, jnp.exp(jax.nn.log_sigmoid(z) + after), 0.0)
        return jnp.einsum('bhqk,bhkd->bhqd', a.astype(v.dtype), v)

    out = lax.map(block, (q_blocks, jnp.arange(nb)))
    return out.transpose(1, 2, 0, 3, 4).reshape(b, h, s, d)


def forgetting_attention(q, k, v, log_f):
    b, h, s, d = q.shape
    nb = s // Q_BLOCK
    scale = d ** -0.5
    kpos = jnp.arange(s)
    c = jnp.cumsum(log_f, axis=-1)
    q_blocks = q.reshape(b, h, nb, Q_BLOCK, d).transpose(2, 0, 1, 3, 4)
    c_blocks = c.reshape(b, h, nb, Q_BLOCK).transpose(2, 0, 1, 3)

    def block(args):
        q_blk, c_blk, i = args
        qpos = i * Q_BLOCK + jnp.arange(Q_BLOCK)
        z = jnp.einsum('bhqd,bhkd->bhqk', q_blk, k).astype(jnp.float32) * scale
        z = z + c_blk[..., :, None] - c[..., None, :]
        causal = kpos[None, :] <= qpos[:, None]
        p = jax.nn.softmax(jnp.where(causal, z, -jnp.inf), axis=-1)
        return jnp.einsum('bhqk,bhkd->bhqd', p.astype(v.dtype), v)

    out = lax.map(block, (q_blocks, c_blocks, jnp.arange(nb)))
    return out.transpose(1, 2, 0, 3, 4).reshape(b, h, s, d)


def setup_inputs(seed: int = 0) -> dict:
    key = jax.random.key(seed)
    ks = jax.random.split(key, 17)

    def dense(k, fan_in, fan_out):
        return jax.random.normal(k, (DEPTH, fan_in, fan_out), jnp.float32) * fan_in ** -0.5

    def gain(k, n):
        return 1.0 + 0.02 * jax.random.normal(k, (DEPTH, n), jnp.float32)

    return {
        "x": jax.random.normal(ks[0], (BATCH, SEQ, D_MODEL), jnp.float32),
        "norm_ffn1_g": gain(ks[1], D_MODEL),
        "ffn1_w_gate": dense(ks[2], D_MODEL, D_FF),
        "ffn1_w_up": dense(ks[3], D_MODEL, D_FF),
        "ffn1_w_down": dense(ks[4], D_FF, D_MODEL),
        "norm_mix_g": gain(ks[5], D_MODEL),
        "w_in": dense(ks[6], D_MODEL, IN_PROJ_DIM),
        "b_f": jax.random.uniform(ks[7], (DEPTH, N_FOX_HEADS), jnp.float32, 1.0, 4.0),
        "sb_out_g": gain(ks[8], SB_WIDTH),
        "fox_out_g": gain(ks[9], FOX_WIDTH),
        "w_o": dense(ks[10], MIX_WIDTH, D_MODEL),
        "norm_ffn2_g": gain(ks[11], D_MODEL),
        "ffn2_w_gate": dense(ks[12], D_MODEL, D_FF),
        "ffn2_w_up": dense(ks[13], D_MODEL, D_FF),
        "ffn2_w_down": dense(ks[14], D_FF, D_MODEL),
        "norm_final_g": 1.0 + 0.02 * jax.random.normal(ks[15], (D_MODEL,), jnp.float32),
    }


def reference(x, norm_ffn1_g, ffn1_w_gate, ffn1_w_up, ffn1_w_down, norm_mix_g, w_in, b_f,
              sb_out_g, fox_out_g, w_o, norm_ffn2_g, ffn2_w_gate, ffn2_w_up, ffn2_w_down,
              norm_final_g):
    for l in range(DEPTH):
        h = rms_norm(x, norm_ffn1_g[l])
        x = x + FFN_RESIDUAL_SCALE * swiglu(h, ffn1_w_gate[l], ffn1_w_up[l], ffn1_w_down[l])

        h = rms_norm(x, norm_mix_g[l])
        proj = jnp.einsum('bsd,de->bse', h, w_in[l])
        qa, ka, va, qb, kb, vb, f_logit = jnp.split(proj, SPLIT_POINTS, axis=-1)

        o_sb = stick_breaking_attention(to_heads(qa, N_SB_HEADS), to_heads(ka, N_SB_HEADS),
                                        to_heads(va, N_SB_HEADS))
        log_f = jax.nn.log_sigmoid((f_logit + b_f[l]).astype(jnp.float32)).transpose(0, 2, 1)
        o_fox = forgetting_attention(to_heads(qb, N_FOX_HEADS), to_heads(kb, N_FOX_HEADS),
                                     to_heads(vb, N_FOX_HEADS), log_f)

        merged = jnp.concatenate([head_rms_norm(o_sb, sb_out_g[l]),
                                  head_rms_norm(o_fox, fox_out_g[l])], axis=-1)
        x = x + jnp.einsum('bse,ed->bsd', merged, w_o[l])

        h = rms_norm(x, norm_ffn2_g[l])
        x = x + FFN_RESIDUAL_SCALE * swiglu(h, ffn2_w_gate[l], ffn2_w_up[l], ffn2_w_down[l])
    return rms_norm(x, norm_final_g)
```

```python
import functools

import jax
import jax.numpy as jnp
from jax import lax
from jax.experimental import pallas as pl
from jax.experimental.pallas import tpu as pltpu

D_MODEL = 4096
HEAD_DIM = 128
N_HEADS = 16
GROUP_WIDTH = N_HEADS * HEAD_DIM
QKV_WIDTH = 6 * GROUP_WIDTH
D_FF = 11008
D_FF_PAD = 11264
EPS = 1e-6
FFN_RESIDUAL_SCALE = 0.5
ATTN_SCALE = HEAD_DIM ** -0.5
NEG_BIG = -1e30
F32_EXP_UNDERFLOW = -104.0

LANES = 128
VMEM_LIMIT_BYTES = 58 * 1024 * 1024

F32 = jnp.float32
BF16 = jnp.bfloat16


def _params(n_axes):
    return pltpu.CompilerParams(
        dimension_semantics=("arbitrary",) * n_axes,
        vmem_limit_bytes=VMEM_LIMIT_BYTES,
    )


def _rmsnorm_kernel(x_ref, g_ref, o_ref):
    x = x_ref[...]
    y = x * lax.rsqrt(jnp.mean(x * x, axis=-1, keepdims=True) + EPS)
    o_ref[...] = (y * g_ref[...]).astype(o_ref.dtype)


def _rmsnorm(x, g, out_dtype, tm=256):
    m, d = x.shape
    return pl.pallas_call(
        _rmsnorm_kernel,
        grid=(m // tm,),
        in_specs=[pl.BlockSpec((tm, d), lambda i: (i, 0)),
                  pl.BlockSpec((1, d), lambda i: (0, 0))],
        out_specs=pl.BlockSpec((tm, d), lambda i: (i, 0)),
        out_shape=jax.ShapeDtypeStruct((m, d), out_dtype),
        compiler_params=_params(1),
        name="rmsnorm",
    )(x, g.reshape(1, d))


def _gateup_kernel(h_ref, w_ref, o_ref, *, tf):
    r = jnp.dot(h_ref[...], w_ref[...], preferred_element_type=F32)
    gate = r[:, :tf]
    up = r[:, tf:]
    o_ref[...] = (gate * (1.0 / (1.0 + jnp.exp(-gate))) * up).astype(o_ref.dtype)


def _gateup(h, w_gu, tm=1024, tf=512):
    m, d = h.shape
    f = w_gu.shape[1] // 2
    return pl.pallas_call(
        functools.partial(_gateup_kernel, tf=tf),
        grid=(m // tm, f // tf),
        in_specs=[pl.BlockSpec((tm, d), lambda i, j: (i, 0)),
                  pl.BlockSpec((d, 2 * tf), lambda i, j: (0, j))],
        out_specs=pl.BlockSpec((tm, tf), lambda i, j: (i, j)),
        out_shape=jax.ShapeDtypeStruct((m, f), BF16),
        compiler_params=_params(2),
        name="ffn_gateup",
    )(h, w_gu)


def _down_kernel(a_ref, w_ref, x_ref, o_ref, acc_ref, *, nk):
    k = pl.program_id(2)

    @pl.when(k == 0)
    def _():
        acc_ref[...] = jnp.zeros_like(acc_ref)

    acc_ref[...] += jnp.dot(a_ref[...], w_ref[...], preferred_element_type=F32)

    @pl.when(k == nk - 1)
    def _():
        o_ref[...] = x_ref[...] + FFN_RESIDUAL_SCALE * acc_ref[...]


def _down(a, w_d, x, tm=1024, tn=1024, tk=2816):
    m, f = a.shape
    n = w_d.shape[1]
    nk = f // tk
    return pl.pallas_call(
        functools.partial(_down_kernel, nk=nk),
        grid=(m // tm, n // tn, nk),
        in_specs=[pl.BlockSpec((tm, tk), lambda i, j, k: (i, k)),
                  pl.BlockSpec((tk, tn), lambda i, j, k: (k, j)),
                  pl.BlockSpec((tm, tn), lambda i, j, k: (i, j))],
        out_specs=pl.BlockSpec((tm, tn), lambda i, j, k: (i, j)),
        out_shape=jax.ShapeDtypeStruct((m, n), F32),
        scratch_shapes=[pltpu.VMEM((tm, tn), F32)],
        compiler_params=_params(3),
        name="ffn_down",
    )(a, w_d, x)


def _prep_ffn_weights(w_gate, w_up, w_down, tf=512):
    d = w_gate.shape[0]
    pad = D_FF_PAD - D_FF
    wg = jnp.pad(w_gate.astype(BF16), ((0, 0), (0, pad))).reshape(d, D_FF_PAD // tf, 1, tf)
    wu = jnp.pad(w_up.astype(BF16), ((0, 0), (0, pad))).reshape(d, D_FF_PAD // tf, 1, tf)
    w_gu = jnp.concatenate([wg, wu], axis=2).reshape(d, 2 * D_FF_PAD)
    w_d = jnp.pad(w_down.astype(BF16), ((0, pad), (0, 0)))
    return w_gu, w_d


def _ffn(x, norm_g, w_gate, w_up, w_down):
    w_gu, w_d = _prep_ffn_weights(w_gate, w_up, w_down)
    h = _rmsnorm(x, norm_g, BF16)
    a = _gateup(h, w_gu)
    return _down(a, w_d, x)


def _inproj_kernel(h_ref, w_ref, o_ref, *, heads_per_block):
    r = jnp.dot(h_ref[...], w_ref[...], preferred_element_type=F32)
    for c in range(heads_per_block):
        o_ref[c] = r[:, c * HEAD_DIM:(c + 1) * HEAD_DIM].astype(o_ref.dtype)


def _inproj(h, w_qkv, tm=1024, tn=1024):
    m, d = h.shape
    n = w_qkv.shape[1]
    hpb = tn // HEAD_DIM
    return pl.pallas_call(
        functools.partial(_inproj_kernel, heads_per_block=hpb),
        grid=(m // tm, n // tn),
        in_specs=[pl.BlockSpec((tm, d), lambda i, j: (i, 0)),
                  pl.BlockSpec((d, tn), lambda i, j: (0, j))],
        out_specs=pl.BlockSpec((hpb, tm, HEAD_DIM), lambda i, j: (j, i, 0)),
        out_shape=jax.ShapeDtypeStruct((n // HEAD_DIM, m, HEAD_DIM), BF16),
        compiler_params=_params(2),
        name="in_proj",
    )(h, w_qkv)


def _split3(x):
    hi = x.astype(BF16)
    r = x - hi.astype(F32)
    mid = r.astype(BF16)
    lo = (r - mid.astype(F32)).astype(BF16)
    return hi, mid, lo


def _forget_kernel(h_ref, wt_ref, b_ref, o_ref, carry_ref, *, ts):
    s = pl.program_id(1)

    @pl.when(s == 0)
    def _():
        carry_ref[...] = jnp.zeros_like(carry_ref)

    logit = lax.dot_general(wt_ref[...], h_ref[...], (((1,), (1,)), ((), ())),
                            preferred_element_type=F32) + b_ref[...]
    log_f = jnp.minimum(logit, 0.0) - jnp.log1p(jnp.exp(-jnp.abs(logit)))
    r = lax.broadcasted_iota(jnp.int32, (ts, ts), 0)
    c = lax.broadcasted_iota(jnp.int32, (ts, ts), 1)
    tri = (r <= c).astype(BF16)
    hi, mid, lo = _split3(log_f)
    cum = (jnp.dot(hi, tri, preferred_element_type=F32)
           + jnp.dot(mid, tri, preferred_element_type=F32)
           + jnp.dot(lo, tri, preferred_element_type=F32)) + carry_ref[...]
    o_ref[...] = cum
    carry_ref[...] = cum[:, ts - 1:ts]


def _forget_cumsum(h, w_f_t, b_col, batch, seq, ts=512):
    d = h.shape[1]
    ns = seq // ts
    return pl.pallas_call(
        functools.partial(_forget_kernel, ts=ts),
        grid=(batch, ns),
        in_specs=[pl.BlockSpec((ts, d), lambda b, s: (b * ns + s, 0)),
                  pl.BlockSpec((LANES, d), lambda b, s: (0, 0)),
                  pl.BlockSpec((LANES, 1), lambda b, s: (0, 0))],
        out_specs=pl.BlockSpec((None, LANES, ts), lambda b, s: (b, 0, s)),
        out_shape=jax.ShapeDtypeStruct((batch, LANES, seq), F32),
        scratch_shapes=[pltpu.VMEM((LANES, 1), F32)],
        compiler_params=_params(2),
        name="forget_cumsum",
    )(h, w_f_t, b_col)


def _head_rmsnorm(o, gain):
    return o * lax.rsqrt(jnp.mean(o * o, axis=-1, keepdims=True) + EPS) * gain


def _dot_nt(a, b):
    return lax.dot_general(a, b, (((1,), (1,)), ((), ())), preferred_element_type=F32)


def _fox_kernel(q_ref, k_ref, v_ref, c_ref, g_ref, o_ref, *, tq):
    i = pl.program_id(2)
    q = q_ref[...]
    c_q0 = c_ref[:, pl.ds(pl.multiple_of(i * tq, tq), LANES)][:, 0:1]

    def block(j, carry, masked):
        m, l, acc = carry
        start = pl.multiple_of(j * tq, tq)
        ks = k_ref[pl.ds(start, tq), :]
        vs = v_ref[pl.ds(start, tq), :]
        s = _dot_nt(q, ks) * ATTN_SCALE + (c_q0 - c_ref[:, pl.ds(start, tq)])
        if masked:
            row = lax.broadcasted_iota(jnp.int32, (tq, tq), 0)
            col = lax.broadcasted_iota(jnp.int32, (tq, tq), 1)
            s = jnp.where(col <= row, s, NEG_BIG)
        m_new = jnp.maximum(m, jnp.max(s, axis=-1, keepdims=True))
        alpha = jnp.exp(m - m_new)
        p = jnp.exp(s - m_new)
        l = alpha * l + jnp.sum(p, axis=-1, keepdims=True)
        acc = alpha * acc + jnp.dot(p.astype(BF16), vs, preferred_element_type=F32)
        return m_new, l, acc

    init = (jnp.full((tq, 1), NEG_BIG, F32), jnp.zeros((tq, 1), F32),
            jnp.zeros((tq, HEAD_DIM), F32))
    carry = lax.fori_loop(0, i, lambda j, c: block(j, c, False), init)
    _, l, acc = block(i, carry, True)
    o_ref[...] = _head_rmsnorm(acc / l, g_ref[...]).astype(o_ref.dtype)


def _sb_kernel(q_ref, k_ref, v_ref, g_ref, o_ref, *, tq):
    i = pl.program_id(2)
    q = q_ref[...]
    row = lax.broadcasted_iota(jnp.int32, (tq, tq), 0)
    col = lax.broadcasted_iota(jnp.int32, (tq, tq), 1)
    past = col < row
    tri = (row > col).astype(BF16)
    tri2 = jnp.concatenate([tri, tri], axis=0)

    def block(j, rem, acc, masked):
        start = pl.multiple_of(j * tq, tq)
        ks = k_ref[pl.ds(start, tq), :]
        vs = v_ref[pl.ds(start, tq), :]
        z = _dot_nt(q, ks) * ATTN_SCALE
        ls_pos = jnp.minimum(z, 0.0) - jnp.log1p(jnp.exp(-jnp.abs(z)))
        log_keep = ls_pos - z
        if masked:
            log_keep = jnp.where(past, log_keep, 0.0)
        hi = log_keep.astype(BF16)
        lo = (log_keep - hi.astype(F32)).astype(BF16)
        after = jnp.dot(jnp.concatenate([hi, lo], axis=1), tri2, preferred_element_type=F32)
        a = jnp.exp(ls_pos + after + rem)
        if masked:
            a = jnp.where(past, a, 0.0)
        acc = acc + jnp.dot(a.astype(BF16), vs, preferred_element_type=F32)
        rem = rem + jnp.sum(log_keep, axis=-1, keepdims=True)
        return rem, acc

    rem, acc = block(i, jnp.zeros((tq, 1), F32), jnp.zeros((tq, HEAD_DIM), F32), True)

    def cond(state):
        j, rem_max, _, _ = state
        return jnp.logical_and(j >= 0, rem_max > F32_EXP_UNDERFLOW)

    def body(state):
        j, _, rem, acc = state
        rem, acc = block(j, rem, acc, False)
        return j - 1, jnp.max(rem), rem, acc

    _, _, _, acc = lax.while_loop(cond, body, (i - 1, jnp.max(rem), rem, acc))
    o_ref[...] = _head_rmsnorm(acc, g_ref[...]).astype(o_ref.dtype)


def _attention(qkvh, c_rows, sb_gain, fox_gain, batch, seq, tq=256):
    nq = seq // tq
    m = batch * seq
    grid = (batch, N_HEADS, nq)

    def q_spec(section):
        return pl.BlockSpec((None, tq, HEAD_DIM), lambda b, h, i: (section * N_HEADS + h, b * nq + i, 0))

    def kv_spec(section):
        return pl.BlockSpec((None, seq, HEAD_DIM), lambda b, h, i: (section * N_HEADS + h, b, 0))

    gain_spec = pl.BlockSpec((None, 1, HEAD_DIM), lambda b, h, i: (h, 0, 0))
    out_spec = pl.BlockSpec((tq, HEAD_DIM), lambda b, h, i: (b * nq + i, h))
    out_shape = jax.ShapeDtypeStruct((m, GROUP_WIDTH), BF16)

    o_sb = pl.pallas_call(
        functools.partial(_sb_kernel, tq=tq),
        grid=grid,
        in_specs=[q_spec(0), kv_spec(1), kv_spec(2), gain_spec],
        out_specs=out_spec,
        out_shape=out_shape,
        compiler_params=_params(3),
        name="sb_attention",
    )(qkvh, qkvh, qkvh, sb_gain.reshape(N_HEADS, 1, HEAD_DIM))

    o_fox = pl.pallas_call(
        functools.partial(_fox_kernel, tq=tq),
        grid=grid,
        in_specs=[q_spec(3), kv_spec(4), kv_spec(5),
                  pl.BlockSpec((None, None, 1, seq), lambda b, h, i: (b, h, 0, 0)),
                  gain_spec],
        out_specs=out_spec,
        out_shape=out_shape,
        compiler_params=_params(3),
        name="fox_attention",
    )(qkvh, qkvh, qkvh, c_rows, fox_gain.reshape(N_HEADS, 1, HEAD_DIM))
    return o_sb, o_fox


def _outproj_kernel(a_ref, b_ref, w_ref, x_ref, o_ref):
    acc = jnp.dot(a_ref[...], w_ref[:GROUP_WIDTH, :], preferred_element_type=F32)
    acc = acc + jnp.dot(b_ref[...], w_ref[GROUP_WIDTH:, :], preferred_element_type=F32)
    o_ref[...] = x_ref[...] + acc


def _outproj(a, b, w_o, x, tm=1024, tn=1024):
    m = a.shape[0]
    n = w_o.shape[1]
    return pl.pallas_call(
        _outproj_kernel,
        grid=(m // tm, n // tn),
        in_specs=[pl.BlockSpec((tm, GROUP_WIDTH), lambda i, j: (i, 0)),
                  pl.BlockSpec((tm, GROUP_WIDTH), lambda i, j: (i, 0)),
                  pl.BlockSpec((2 * GROUP_WIDTH, tn), lambda i, j: (0, j)),
                  pl.BlockSpec((tm, tn), lambda i, j: (i, j))],
        out_specs=pl.BlockSpec((tm, tn), lambda i, j: (i, j)),
        out_shape=jax.ShapeDtypeStruct((m, n), F32),
        compiler_params=_params(2),
        name="out_proj",
    )(a, b, w_o, x)


def _mixer(x, norm_g, w_in, b_f, sb_g, fox_g, w_o, batch, seq):
    h = _rmsnorm(x, norm_g, BF16)
    qkvh = _inproj(h, w_in[:, :QKV_WIDTH].astype(BF16))
    w_f_t = jnp.zeros((LANES, D_MODEL), BF16).at[:N_HEADS].set(w_in[:, QKV_WIDTH:].T.astype(BF16))
    b_col = jnp.zeros((LANES, 1), F32).at[:N_HEADS, 0].set(b_f)
    c = _forget_cumsum(h, w_f_t, b_col, batch, seq)
    c_rows = c[:, :N_HEADS, :].reshape(batch, N_HEADS, 1, seq)
    o_sb, o_fox = _attention(qkvh, c_rows, sb_g, fox_g, batch, seq)
    return _outproj(o_sb, o_fox, w_o.astype(BF16), x)


@jax.jit
def kernel(x, norm_ffn1_g, ffn1_w_gate, ffn1_w_up, ffn1_w_down, norm_mix_g, w_in, b_f,
           sb_out_g, fox_out_g, w_o, norm_ffn2_g, ffn2_w_gate, ffn2_w_up, ffn2_w_down,
           norm_final_g):
    batch, seq, d = x.shape
    depth = w_in.shape[0]
    xf = x.reshape(batch * seq, d)
    for l in range(depth):
        xf = _ffn(xf, norm_ffn1_g[l], ffn1_w_gate[l], ffn1_w_up[l], ffn1_w_down[l])
        xf = _mixer(xf, norm_mix_g[l], w_in[l], b_f[l], sb_out_g[l], fox_out_g[l], w_o[l], batch, seq)
        xf = _ffn(xf, norm_ffn2_g[l], ffn2_w_gate[l], ffn2_w_up[l], ffn2_w_down[l])
    out = _rmsnorm(xf, norm_final_g, F32)
    return out.reshape(batch, seq, d)
```

```python
import functools
import math

import jax
import jax.numpy as jnp
from jax import lax
from jax.experimental import pallas as pl
from jax.experimental.pallas import tpu as pltpu

D_MODEL = 4096
HEAD_DIM = 128
N_HEADS = 16
GROUP_WIDTH = N_HEADS * HEAD_DIM
QKV_WIDTH = 6 * GROUP_WIDTH
EPS = 1e-6
FFN_RESIDUAL_SCALE = 0.5
ATTN_SCALE = HEAD_DIM ** -0.5
LOG2E = math.log2(math.e)
NEG_BIG = -1e30
F32_EXP_UNDERFLOW = -104.0

LANES = 128
VMEM_LIMIT_BYTES = 58 * 1024 * 1024

F32 = jnp.float32
BF16 = jnp.bfloat16


def _params(n_axes):
    return pltpu.CompilerParams(
        dimension_semantics=("arbitrary",) * n_axes,
        vmem_limit_bytes=VMEM_LIMIT_BYTES,
    )


def _rmsnorm_kernel(x_ref, g_ref, o_ref):
    x = x_ref[...]
    y = x * lax.rsqrt(jnp.mean(x * x, axis=-1, keepdims=True) + EPS)
    o_ref[...] = (y * g_ref[...]).astype(o_ref.dtype)


def _rmsnorm(x, g, out_dtype, tm=256):
    m, d = x.shape
    return pl.pallas_call(
        _rmsnorm_kernel,
        grid=(m // tm,),
        in_specs=[pl.BlockSpec((tm, d), lambda i: (i, 0)),
                  pl.BlockSpec((1, d), lambda i: (0, 0))],
        out_specs=pl.BlockSpec((tm, d), lambda i: (i, 0)),
        out_shape=jax.ShapeDtypeStruct((m, d), out_dtype),
        compiler_params=_params(1),
        name="rmsnorm",
    )(x, g.reshape(1, d))


def _gateup_kernel(h_ref, wg_ref, wu_ref, o_ref, wg_bf, wu_bf, *, tf, tail):
    j = pl.program_id(0)
    i = pl.program_id(1)
    nj = pl.num_programs(0)

    def step(width):
        @pl.when(i == 0)
        def _():
            wg_bf[:, :width] = wg_ref[:, :width].astype(BF16)
            wu_bf[:, :width] = wu_ref[:, :width].astype(BF16)

        h = h_ref[...]
        gate = jnp.dot(h, wg_bf[:, :width], preferred_element_type=F32)
        up = jnp.dot(h, wu_bf[:, :width], preferred_element_type=F32)
        o_ref[:, :width] = (gate * (1.0 / (1.0 + jnp.exp(-gate))) * up).astype(o_ref.dtype)

    if tail == tf:
        step(tf)
    else:
        pl.when(j < nj - 1)(lambda: step(tf))
        pl.when(j == nj - 1)(lambda: step(tail))


def _gateup(h, w_gate, w_up, tm=1024, tf=512):
    m, d = h.shape
    f = w_gate.shape[1]
    nj = pl.cdiv(f, tf)
    tail = f - (nj - 1) * tf
    w_spec = pl.BlockSpec((d, tf), lambda j, i: (0, j), pipeline_mode=pl.Buffered(1))
    return pl.pallas_call(
        functools.partial(_gateup_kernel, tf=tf, tail=tail),
        grid=(nj, m // tm),
        in_specs=[pl.BlockSpec((tm, d), lambda j, i: (i, 0)), w_spec, w_spec],
        out_specs=pl.BlockSpec((tm, tf), lambda j, i: (i, j)),
        out_shape=jax.ShapeDtypeStruct((m, f), BF16),
        scratch_shapes=[pltpu.VMEM((d, tf), BF16), pltpu.VMEM((d, tf), BF16)],
        compiler_params=_params(2),
        name="ffn_gateup",
    )(h, w_gate, w_up)


def _down_kernel(a_ref, w_ref, x_ref, o_ref, acc_ref, *, nk, tail):
    k = pl.program_id(2)

    @pl.when(k == 0)
    def _():
        acc_ref[...] = jnp.zeros_like(acc_ref)

    @pl.when(k < nk - 1)
    def _():
        acc_ref[...] += jnp.dot(a_ref[...], w_ref[...], preferred_element_type=F32)

    @pl.when(k == nk - 1)
    def _():
        last = jnp.dot(a_ref[:, :tail], w_ref[:tail, :], preferred_element_type=F32)
        o_ref[...] = x_ref[...] + FFN_RESIDUAL_SCALE * (acc_ref[...] + last)


def _down(a, w_d, x, tm=1024, tn=1024, tk=2816):
    m, f = a.shape
    n = w_d.shape[1]
    nk = pl.cdiv(f, tk)
    tail = f - (nk - 1) * tk
    return pl.pallas_call(
        functools.partial(_down_kernel, nk=nk, tail=tail),
        grid=(m // tm, n // tn, nk),
        in_specs=[pl.BlockSpec((tm, tk), lambda i, j, k: (i, k)),
                  pl.BlockSpec((tk, tn), lambda i, j, k: (k, j)),
                  pl.BlockSpec((tm, tn), lambda i, j, k: (i, j))],
        out_specs=pl.BlockSpec((tm, tn), lambda i, j, k: (i, j)),
        out_shape=jax.ShapeDtypeStruct((m, n), F32),
        scratch_shapes=[pltpu.VMEM((tm, tn), F32)],
        compiler_params=_params(3),
        name="ffn_down",
    )(a, w_d, x)


def _ffn(x, norm_g, w_gate, w_up, w_down):
    h = _rmsnorm(x, norm_g, BF16)
    a = _gateup(h, w_gate, w_up)
    return _down(a, w_down.astype(BF16), x)


def _inproj_kernel(h_ref, w_ref, o_ref, w_bf, *, heads_per_block):
    i = pl.program_id(1)

    @pl.when(i == 0)
    def _():
        w_bf[...] = w_ref[...].astype(BF16)

    r = jnp.dot(h_ref[...], w_bf[...], preferred_element_type=F32)
    for c in range(heads_per_block):
        o_ref[c] = r[:, c * HEAD_DIM:(c + 1) * HEAD_DIM].astype(o_ref.dtype)


def _inproj(h, w_in, tm=1024, tn=1024):
    m, d = h.shape
    hpb = tn // HEAD_DIM
    return pl.pallas_call(
        functools.partial(_inproj_kernel, heads_per_block=hpb),
        grid=(QKV_WIDTH // tn, m // tm),
        in_specs=[pl.BlockSpec((tm, d), lambda j, i: (i, 0)),
                  pl.BlockSpec((d, tn), lambda j, i: (0, j), pipeline_mode=pl.Buffered(1))],
        out_specs=pl.BlockSpec((hpb, tm, HEAD_DIM), lambda j, i: (j, i, 0)),
        out_shape=jax.ShapeDtypeStruct((QKV_WIDTH // HEAD_DIM, m, HEAD_DIM), BF16),
        scratch_shapes=[pltpu.VMEM((d, tn), BF16)],
        compiler_params=_params(2),
        name="in_proj",
    )(h, w_in)


def _split3(x):
    hi = x.astype(BF16)
    r = x - hi.astype(F32)
    mid = r.astype(BF16)
    lo = (r - mid.astype(F32)).astype(BF16)
    return hi, mid, lo


def _forget_kernel(h_ref, wt_ref, b_ref, o_ref, carry_ref, *, ts):
    s = pl.program_id(1)

    @pl.when(s == 0)
    def _():
        carry_ref[...] = jnp.zeros_like(carry_ref)

    logit = lax.dot_general(wt_ref[...], h_ref[...], (((1,), (1,)), ((), ())),
                            preferred_element_type=F32) + b_ref[...]
    log_f = jnp.minimum(logit, 0.0) - jnp.log1p(jnp.exp(-jnp.abs(logit)))
    r = lax.broadcasted_iota(jnp.int32, (ts, ts), 0)
    c = lax.broadcasted_iota(jnp.int32, (ts, ts), 1)
    tri = (r <= c).astype(BF16)
    hi, mid, lo = _split3(log_f)
    cum = (jnp.dot(hi, tri, preferred_element_type=F32)
           + jnp.dot(mid, tri, preferred_element_type=F32)
           + jnp.dot(lo, tri, preferred_element_type=F32)) + carry_ref[...]
    o_ref[...] = cum
    carry_ref[...] = cum[:, ts - 1:ts]


def _forget_cumsum(h, w_f_t, b_col, batch, seq, ts=512):
    d = h.shape[1]
    ns = seq // ts
    return pl.pallas_call(
        functools.partial(_forget_kernel, ts=ts),
        grid=(batch, ns),
        in_specs=[pl.BlockSpec((ts, d), lambda b, s: (b * ns + s, 0)),
                  pl.BlockSpec((LANES, d), lambda b, s: (0, 0)),
                  pl.BlockSpec((LANES, 1), lambda b, s: (0, 0))],
        out_specs=pl.BlockSpec((None, LANES, ts), lambda b, s: (b, 0, s)),
        out_shape=jax.ShapeDtypeStruct((batch, LANES, seq), F32),
        scratch_shapes=[pltpu.VMEM((LANES, 1), F32)],
        compiler_params=_params(2),
        name="forget_cumsum",
    )(h, w_f_t, b_col)


def _head_rmsnorm(o, gain):
    return o * lax.rsqrt(jnp.mean(o * o, axis=-1, keepdims=True) + EPS) * gain


def _dot_nt(a, b):
    return lax.dot_general(a, b, (((1,), (1,)), ((), ())), preferred_element_type=F32)


def _fox_kernel(q_ref, k_ref, v_ref, c_ref, g_ref, o_ref, *, t):
    i = pl.program_id(2)
    q = q_ref[...]
    c_q0 = c_ref[:, pl.ds(pl.multiple_of(i * t, t), LANES)][:, 0:1]

    def scores(j):
        start = pl.multiple_of(j * t, t)
        bias = (c_q0 - c_ref[:, pl.ds(start, t)]) * LOG2E
        return _dot_nt(q, k_ref[pl.ds(start, t), :]) * (ATTN_SCALE * LOG2E) + bias

    def update(j, s, m, l, acc):
        vs = v_ref[pl.ds(pl.multiple_of(j * t, t), t), :]
        m_new = jnp.maximum(m, jnp.max(s, axis=-1, keepdims=True))
        alpha = jnp.exp2(m - m_new)
        p = jnp.exp2(s - m_new)
        l = alpha * l + jnp.sum(p, axis=-1, keepdims=True)
        acc = alpha * acc + jnp.dot(p.astype(BF16), vs, preferred_element_type=F32)
        return m_new, l, acc

    def body(j, carry):
        s, m, l, acc = carry
        s_next = scores(j + 1)
        m, l, acc = update(j, s, m, l, acc)
        return s_next, m, l, acc

    init = (scores(0), jnp.full((t, 1), NEG_BIG, F32), jnp.zeros((t, 1), F32),
            jnp.zeros((t, HEAD_DIM), F32))
    s, m, l, acc = lax.fori_loop(0, i, body, init)
    row = lax.broadcasted_iota(jnp.int32, (t, t), 0)
    col = lax.broadcasted_iota(jnp.int32, (t, t), 1)
    _, l, acc = update(i, jnp.where(col <= row, s, NEG_BIG), m, l, acc)
    o_ref[...] = _head_rmsnorm(acc / l, g_ref[...]).astype(o_ref.dtype)


def _sb_kernel(q_ref, k_ref, v_ref, g_ref, o_ref, *, t):
    i = pl.program_id(2)
    q = q_ref[...]
    row = lax.broadcasted_iota(jnp.int32, (t, t), 0)
    col = lax.broadcasted_iota(jnp.int32, (t, t), 1)
    past = col < row
    tri = (row > col).astype(BF16)
    tri2 = jnp.concatenate([tri, tri], axis=0)

    def block(j, rem, acc, masked):
        start = pl.multiple_of(j * t, t)
        ks = k_ref[pl.ds(start, t), :]
        vs = v_ref[pl.ds(start, t), :]
        z = _dot_nt(q, ks) * ATTN_SCALE
        ls_pos = jnp.minimum(z, 0.0) - jnp.log1p(jnp.exp(-jnp.abs(z)))
        log_keep = ls_pos - z
        if masked:
            log_keep = jnp.where(past, log_keep, 0.0)
        hi = log_keep.astype(BF16)
        lo = (log_keep - hi.astype(F32)).astype(BF16)
        after = jnp.dot(jnp.concatenate([hi, lo], axis=1), tri2, preferred_element_type=F32)
        a = jnp.exp(ls_pos + after + rem)
        if masked:
            a = jnp.where(past, a, 0.0)
        acc = acc + jnp.dot(a.astype(BF16), vs, preferred_element_type=F32)
        rem = rem + jnp.sum(log_keep, axis=-1, keepdims=True)
        return rem, acc

    rem, acc = block(i, jnp.zeros((t, 1), F32), jnp.zeros((t, HEAD_DIM), F32), True)

    def cond(state):
        j, rem_max, _, _ = state
        return jnp.logical_and(j >= 0, rem_max > F32_EXP_UNDERFLOW)

    def body(state):
        j, _, rem, acc = state
        rem, acc = block(j, rem, acc, False)
        return j - 1, jnp.max(rem), rem, acc

    _, _, _, acc = lax.while_loop(cond, body, (i - 1, jnp.max(rem), rem, acc))
    o_ref[...] = _head_rmsnorm(acc, g_ref[...]).astype(o_ref.dtype)


def _attention_call(kernel_fn, name, qkvh, first_section, extra_inputs, extra_specs, gain, batch, seq, t):
    nq = seq // t

    def section_spec(section, rows):
        blocks_per_batch = seq // rows
        return pl.BlockSpec(
            (None, rows, HEAD_DIM),
            lambda b, h, i: (section * N_HEADS + h, b * blocks_per_batch + (i if rows == t else 0), 0))

    return pl.pallas_call(
        functools.partial(kernel_fn, t=t),
        grid=(batch, N_HEADS, nq),
        in_specs=[section_spec(first_section, t), section_spec(first_section + 1, seq),
                  section_spec(first_section + 2, seq), *extra_specs,
                  pl.BlockSpec((None, 1, HEAD_DIM), lambda b, h, i: (h, 0, 0))],
        out_specs=pl.BlockSpec((t, HEAD_DIM), lambda b, h, i: (b * nq + i, h)),
        out_shape=jax.ShapeDtypeStruct((batch * seq, GROUP_WIDTH), BF16),
        compiler_params=_params(3),
        name=name,
    )(qkvh, qkvh, qkvh, *extra_inputs, gain.reshape(N_HEADS, 1, HEAD_DIM))


def _attention(qkvh, c_rows, sb_gain, fox_gain, batch, seq, t_sb=256, t_fox=512):
    o_sb = _attention_call(_sb_kernel, "sb_attention", qkvh, 0, (), (), sb_gain, batch, seq, t_sb)
    c_spec = pl.BlockSpec((None, None, 1, seq), lambda b, h, i: (b, h, 0, 0))
    o_fox = _attention_call(_fox_kernel, "fox_attention", qkvh, 3, (c_rows,), (c_spec,), fox_gain,
                            batch, seq, t_fox)
    return o_sb, o_fox


def _outproj_kernel(a_ref, b_ref, w_ref, x_ref, o_ref):
    acc = jnp.dot(a_ref[...], w_ref[:GROUP_WIDTH, :], preferred_element_type=F32)
    acc = acc + jnp.dot(b_ref[...], w_ref[GROUP_WIDTH:, :], preferred_element_type=F32)
    o_ref[...] = x_ref[...] + acc


def _outproj(a, b, w_o, x, tm=1024, tn=1024):
    m = a.shape[0]
    n = w_o.shape[1]
    return pl.pallas_call(
        _outproj_kernel,
        grid=(m // tm, n // tn),
        in_specs=[pl.BlockSpec((tm, GROUP_WIDTH), lambda i, j: (i, 0)),
                  pl.BlockSpec((tm, GROUP_WIDTH), lambda i, j: (i, 0)),
                  pl.BlockSpec((2 * GROUP_WIDTH, tn), lambda i, j: (0, j)),
                  pl.BlockSpec((tm, tn), lambda i, j: (i, j))],
        out_specs=pl.BlockSpec((tm, tn), lambda i, j: (i, j)),
        out_shape=jax.ShapeDtypeStruct((m, n), F32),
        compiler_params=_params(2),
        name="out_proj",
    )(a, b, w_o, x)


def _mixer(x, norm_g, w_in, b_f, sb_g, fox_g, w_o, batch, seq):
    h = _rmsnorm(x, norm_g, BF16)
    qkvh = _inproj(h, w_in)
    w_f_t = jnp.zeros((LANES, D_MODEL), BF16).at[:N_HEADS].set(w_in[:, QKV_WIDTH:].T.astype(BF16))
    b_col = jnp.zeros((LANES, 1), F32).at[:N_HEADS, 0].set(b_f)
    c = _forget_cumsum(h, w_f_t, b_col, batch, seq)
    c_rows = c[:, :N_HEADS, :].reshape(batch, N_HEADS, 1, seq)
    o_sb, o_fox = _attention(qkvh, c_rows, sb_g, fox_g, batch, seq)
    return _outproj(o_sb, o_fox, w_o.astype(BF16), x)


@jax.jit
def kernel(x, norm_ffn1_g, ffn1_w_gate, ffn1_w_up, ffn1_w_down, norm_mix_g, w_in, b_f,
           sb_out_g, fox_out_g, w_o, norm_ffn2_g, ffn2_w_gate, ffn2_w_up, ffn2_w_down,
           norm_final_g):
    batch, seq, d = x.shape
    depth = w_in.shape[0]
    xf = x.reshape(batch * seq, d)
    for l in range(depth):
        xf = _ffn(xf, norm_ffn1_g[l], ffn1_w_gate[l], ffn1_w_up[l], ffn1_w_down[l])
        xf = _mixer(xf, norm_mix_g[l], w_in[l], b_f[l], sb_out_g[l], fox_out_g[l], w_o[l], batch, seq)
        xf = _ffn(xf, norm_ffn2_g[l], ffn2_w_gate[l], ffn2_w_up[l], ffn2_w_down[l])
    out = _rmsnorm(xf, norm_final_g, F32)
    return out.reshape(batch, seq, d)
```

```python
import functools
import math

import jax
import jax.numpy as jnp
from jax import lax
from jax.experimental import pallas as pl
from jax.experimental.pallas import tpu as pltpu

D_MODEL = 4096
HEAD_DIM = 128
N_HEADS = 16
GROUP_WIDTH = N_HEADS * HEAD_DIM
QKV_WIDTH = 6 * GROUP_WIDTH
EPS = 1e-6
FFN_RESIDUAL_SCALE = 0.5
ATTN_SCALE = HEAD_DIM ** -0.5
LOG2E = math.log2(math.e)
NEG_BIG = -1e30
F32_EXP2_UNDERFLOW = -152.0

LANES = 128
VMEM_LIMIT_BYTES = 58 * 1024 * 1024

F32 = jnp.float32
BF16 = jnp.bfloat16


def _params(n_axes):
    return pltpu.CompilerParams(
        dimension_semantics=("arbitrary",) * n_axes,
        vmem_limit_bytes=VMEM_LIMIT_BYTES,
    )


def _rmsnorm_kernel(x_ref, g_ref, o_ref):
    x = x_ref[...]
    y = x * lax.rsqrt(jnp.mean(x * x, axis=-1, keepdims=True) + EPS)
    o_ref[...] = (y * g_ref[...]).astype(o_ref.dtype)


def _rmsnorm(x, g, out_dtype, tm=256):
    m, d = x.shape
    return pl.pallas_call(
        _rmsnorm_kernel,
        grid=(m // tm,),
        in_specs=[pl.BlockSpec((tm, d), lambda i: (i, 0)),
                  pl.BlockSpec((1, d), lambda i: (0, 0))],
        out_specs=pl.BlockSpec((tm, d), lambda i: (i, 0)),
        out_shape=jax.ShapeDtypeStruct((m, d), out_dtype),
        compiler_params=_params(1),
        name="rmsnorm",
    )(x, g.reshape(1, d))


def _gateup_kernel(h_ref, wg_ref, wu_ref, o_ref, wg_bf, wu_bf, *, tf, tail):
    j = pl.program_id(0)
    i = pl.program_id(1)
    nj = pl.num_programs(0)

    def step(width):
        @pl.when(i == 0)
        def _():
            wg_bf[:, :width] = wg_ref[:, :width].astype(BF16)
            wu_bf[:, :width] = wu_ref[:, :width].astype(BF16)

        h = h_ref[...]
        gate = jnp.dot(h, wg_bf[:, :width], preferred_element_type=F32)
        up = jnp.dot(h, wu_bf[:, :width], preferred_element_type=F32)
        o_ref[:, :width] = (gate * (1.0 / (1.0 + jnp.exp(-gate))) * up).astype(o_ref.dtype)

    if tail == tf:
        step(tf)
    else:
        pl.when(j < nj - 1)(lambda: step(tf))
        pl.when(j == nj - 1)(lambda: step(tail))


def _gateup(h, w_gate, w_up, tm=1024, tf=512):
    m, d = h.shape
    f = w_gate.shape[1]
    nj = pl.cdiv(f, tf)
    tail = f - (nj - 1) * tf
    w_spec = pl.BlockSpec((d, tf), lambda j, i: (0, j), pipeline_mode=pl.Buffered(1))
    return pl.pallas_call(
        functools.partial(_gateup_kernel, tf=tf, tail=tail),
        grid=(nj, m // tm),
        in_specs=[pl.BlockSpec((tm, d), lambda j, i: (i, 0)), w_spec, w_spec],
        out_specs=pl.BlockSpec((tm, tf), lambda j, i: (i, j)),
        out_shape=jax.ShapeDtypeStruct((m, f), BF16),
        scratch_shapes=[pltpu.VMEM((d, tf), BF16), pltpu.VMEM((d, tf), BF16)],
        compiler_params=_params(2),
        name="ffn_gateup",
    )(h, w_gate, w_up)


def _down_kernel(a_ref, w_ref, x_ref, o_ref, acc_ref, *, nk, tail):
    k = pl.program_id(2)

    @pl.when(k == 0)
    def _():
        acc_ref[...] = jnp.zeros_like(acc_ref)

    @pl.when(k < nk - 1)
    def _():
        acc_ref[...] += jnp.dot(a_ref[...], w_ref[...], preferred_element_type=F32)

    @pl.when(k == nk - 1)
    def _():
        last = jnp.dot(a_ref[:, :tail], w_ref[:tail, :], preferred_element_type=F32)
        o_ref[...] = x_ref[...] + FFN_RESIDUAL_SCALE * (acc_ref[...] + last)


def _down(a, w_d, x, tm=1024, tn=1024, tk=2816):
    m, f = a.shape
    n = w_d.shape[1]
    nk = pl.cdiv(f, tk)
    tail = f - (nk - 1) * tk
    return pl.pallas_call(
        functools.partial(_down_kernel, nk=nk, tail=tail),
        grid=(m // tm, n // tn, nk),
        in_specs=[pl.BlockSpec((tm, tk), lambda i, j, k: (i, k)),
                  pl.BlockSpec((tk, tn), lambda i, j, k: (k, j)),
                  pl.BlockSpec((tm, tn), lambda i, j, k: (i, j))],
        out_specs=pl.BlockSpec((tm, tn), lambda i, j, k: (i, j)),
        out_shape=jax.ShapeDtypeStruct((m, n), F32),
        scratch_shapes=[pltpu.VMEM((tm, tn), F32)],
        compiler_params=_params(3),
        name="ffn_down",
    )(a, w_d, x)


def _ffn(x, norm_g, w_gate, w_up, w_down):
    h = _rmsnorm(x, norm_g, BF16)
    a = _gateup(h, w_gate, w_up)
    return _down(a, w_down.astype(BF16), x)


def _inproj_kernel(h_ref, w_ref, o_ref, w_bf, *, heads_per_block):
    i = pl.program_id(1)

    @pl.when(i == 0)
    def _():
        w_bf[...] = w_ref[...].astype(BF16)

    r = jnp.dot(h_ref[...], w_bf[...], preferred_element_type=F32)
    for c in range(heads_per_block):
        o_ref[c] = r[:, c * HEAD_DIM:(c + 1) * HEAD_DIM].astype(o_ref.dtype)


def _inproj(h, w_in, tm=1024, tn=1024):
    m, d = h.shape
    hpb = tn // HEAD_DIM
    return pl.pallas_call(
        functools.partial(_inproj_kernel, heads_per_block=hpb),
        grid=(QKV_WIDTH // tn, m // tm),
        in_specs=[pl.BlockSpec((tm, d), lambda j, i: (i, 0)),
                  pl.BlockSpec((d, tn), lambda j, i: (0, j), pipeline_mode=pl.Buffered(1))],
        out_specs=pl.BlockSpec((hpb, tm, HEAD_DIM), lambda j, i: (j, i, 0)),
        out_shape=jax.ShapeDtypeStruct((QKV_WIDTH // HEAD_DIM, m, HEAD_DIM), BF16),
        scratch_shapes=[pltpu.VMEM((d, tn), BF16)],
        compiler_params=_params(2),
        name="in_proj",
    )(h, w_in)


def _split3(x):
    hi = x.astype(BF16)
    r = x - hi.astype(F32)
    mid = r.astype(BF16)
    lo = (r - mid.astype(F32)).astype(BF16)
    return hi, mid, lo


def _forget_kernel(h_ref, w_ref, b_ref, o_ref, w_bf, carry_ref, *, ts, n_valid):
    s = pl.program_id(1)
    valid = lax.broadcasted_iota(jnp.int32, (1, LANES), 1) < n_valid

    @pl.when(jnp.logical_and(pl.program_id(0) == 0, s == 0))
    def _():
        w_bf[...] = jnp.where(valid, w_ref[...], 0.0).astype(BF16)

    @pl.when(s == 0)
    def _():
        carry_ref[...] = jnp.zeros_like(carry_ref)

    logit = jnp.dot(h_ref[...], w_bf[...], preferred_element_type=F32) + b_ref[...]
    log_f = (jnp.minimum(logit, 0.0) - jnp.log1p(jnp.exp(-jnp.abs(logit)))).T
    r = lax.broadcasted_iota(jnp.int32, (ts, ts), 0)
    c = lax.broadcasted_iota(jnp.int32, (ts, ts), 1)
    tri = (r <= c).astype(BF16)
    hi, mid, lo = _split3(log_f)
    cum = (jnp.dot(hi, tri, preferred_element_type=F32)
           + jnp.dot(mid, tri, preferred_element_type=F32)
           + jnp.dot(lo, tri, preferred_element_type=F32)) + carry_ref[...]
    o_ref[...] = cum
    carry_ref[...] = cum[:, ts - 1:ts]


def _forget_cumsum(h, w_in, b_row, batch, seq, ts=512):
    d = h.shape[1]
    ns = seq // ts
    n_valid = w_in.shape[1] - QKV_WIDTH
    return pl.pallas_call(
        functools.partial(_forget_kernel, ts=ts, n_valid=n_valid),
        grid=(batch, ns),
        in_specs=[pl.BlockSpec((ts, d), lambda b, s: (b * ns + s, 0)),
                  pl.BlockSpec((d, LANES), lambda b, s: (0, QKV_WIDTH // LANES)),
                  pl.BlockSpec((1, LANES), lambda b, s: (0, 0))],
        out_specs=pl.BlockSpec((None, LANES, ts), lambda b, s: (b, 0, s)),
        out_shape=jax.ShapeDtypeStruct((batch, LANES, seq), F32),
        scratch_shapes=[pltpu.VMEM((d, LANES), BF16), pltpu.VMEM((LANES, 1), F32)],
        compiler_params=_params(2),
        name="forget_cumsum",
    )(h, w_in, b_row)


def _head_rmsnorm(o, gain):
    return o * lax.rsqrt(jnp.mean(o * o, axis=-1, keepdims=True) + EPS) * gain


def _dot_nt(a, b):
    return lax.dot_general(a, b, (((1,), (1,)), ((), ())), preferred_element_type=F32)


def _fox_kernel(q_ref, k_ref, v_ref, c_ref, g_ref, o_ref, vaug_ref, s_ref, m_ref, acc_ref, *, t, tk):
    i = pl.program_id(2)

    @pl.when(i == 0)
    def _():
        vaug_ref[:, :HEAD_DIM] = v_ref[...]
        vaug_ref[:, HEAD_DIM:] = jnp.ones(v_ref.shape, BF16)

    q = q_ref[...]
    c_q0 = c_ref[:, pl.ds(pl.multiple_of(i * t, t), LANES)][:, 0:1]
    n_blocks = ((i + 1) * t + tk - 1) // tk
    last = n_blocks - 1

    def scores(j):
        start = pl.multiple_of(j * tk, tk)
        bias = (c_q0 - c_ref[:, pl.ds(start, tk)]) * LOG2E
        return _dot_nt(q, k_ref[pl.ds(start, tk), :]) * (ATTN_SCALE * LOG2E) + bias

    def step(j, cur, masked):
        if not masked:
            s_ref[1 - cur] = scores(j + 1)
        s = s_ref[cur]
        if masked:
            row = i * t + lax.broadcasted_iota(jnp.int32, (t, tk), 0)
            col = j * tk + lax.broadcasted_iota(jnp.int32, (t, tk), 1)
            s = jnp.where(col <= row, s, NEG_BIG)
        m_old = m_ref[...]
        m_new = jnp.maximum(m_old, jnp.max(s, axis=-1, keepdims=True))
        p = jnp.exp2(s - m_new).astype(BF16)
        vs = vaug_ref[pl.ds(pl.multiple_of(j * tk, tk), tk), :]
        acc_ref[...] = jnp.exp2(m_old - m_new) * acc_ref[...] + jnp.dot(p, vs, preferred_element_type=F32)
        m_ref[...] = m_new

    def either_half(j, masked):
        pl.when(j % 2 == 0)(lambda: step(j, 0, masked))
        pl.when(j % 2 == 1)(lambda: step(j, 1, masked))

    m_ref[...] = jnp.full(m_ref.shape, NEG_BIG, F32)
    acc_ref[...] = jnp.zeros(acc_ref.shape, F32)
    s_ref[0] = scores(0)

    @pl.loop(0, last)
    def _(j):
        either_half(j, False)

    either_half(last, True)
    acc = acc_ref[...]
    o_ref[...] = _head_rmsnorm(acc[:, :HEAD_DIM] / acc[:, HEAD_DIM:], g_ref[...]).astype(o_ref.dtype)


def _sb_kernel(q_ref, k_ref, v_ref, g_ref, o_ref, rem_ref, acc_ref, *, t, tk):
    i = pl.program_id(2)
    q = q_ref[...]
    tri = (lax.broadcasted_iota(jnp.int32, (tk, tk), 0)
           > lax.broadcasted_iota(jnp.int32, (tk, tk), 1)).astype(BF16)
    tri2 = jnp.concatenate([tri, tri], axis=0)
    rem_ref[...] = jnp.zeros(rem_ref.shape, F32)
    acc_ref[...] = jnp.zeros(acc_ref.shape, F32)

    def block(j, masked):
        start = pl.multiple_of(j * tk, tk)
        z = _dot_nt(q, k_ref[pl.ds(start, tk), :]) * (ATTN_SCALE * LOG2E)
        ls_pos = jnp.minimum(z, 0.0) - jnp.log2(1.0 + jnp.exp2(-jnp.abs(z)))
        log_keep = ls_pos - z
        if masked:
            past = (j * tk + lax.broadcasted_iota(jnp.int32, (t, tk), 1)
                    < i * t + lax.broadcasted_iota(jnp.int32, (t, tk), 0))
            log_keep = jnp.where(past, log_keep, 0.0)
        hi = log_keep.astype(BF16)
        lo = (log_keep - hi.astype(F32)).astype(BF16)
        after = jnp.dot(jnp.concatenate([hi, lo], axis=1), tri2, preferred_element_type=F32)
        rem = rem_ref[...]
        a = jnp.exp2(ls_pos + after + rem)
        if masked:
            a = jnp.where(past, a, 0.0)
        acc_ref[...] += jnp.dot(a.astype(BF16), v_ref[pl.ds(start, tk), :], preferred_element_type=F32)
        rem = rem + jnp.sum(log_keep, axis=-1, keepdims=True)
        rem_ref[...] = rem
        return jnp.max(rem)

    n_diag = t // tk
    first = (i + 1) * n_diag - 1
    for d in range(n_diag):
        rem_max = block(first - d, True)

    def cond(state):
        j, rem_max = state
        return jnp.logical_and(j >= 0, rem_max > F32_EXP2_UNDERFLOW)

    def body(state):
        j, _ = state
        return j - 1, block(j, False)

    lax.while_loop(cond, body, (first - n_diag, rem_max))
    o_ref[...] = _head_rmsnorm(acc_ref[...], g_ref[...]).astype(o_ref.dtype)


def _attention_call(kernel_fn, name, qkvh, first_section, extra_inputs, extra_specs, gain, batch, seq, t,
                    scratch_shapes=()):
    nq = seq // t

    def section_spec(section, rows):
        blocks_per_batch = seq // rows
        return pl.BlockSpec(
            (None, rows, HEAD_DIM),
            lambda b, h, i: (section * N_HEADS + h, b * blocks_per_batch + (i if rows == t else 0), 0))

    return pl.pallas_call(
        kernel_fn,
        grid=(batch, N_HEADS, nq),
        in_specs=[section_spec(first_section, t), section_spec(first_section + 1, seq),
                  section_spec(first_section + 2, seq), *extra_specs,
                  pl.BlockSpec((None, 1, HEAD_DIM), lambda b, h, i: (h, 0, 0))],
        out_specs=pl.BlockSpec((t, HEAD_DIM), lambda b, h, i: (b * nq + i, h)),
        out_shape=jax.ShapeDtypeStruct((batch * seq, GROUP_WIDTH), BF16),
        scratch_shapes=list(scratch_shapes),
        compiler_params=_params(3),
        name=name,
    )(qkvh, qkvh, qkvh, *extra_inputs, gain.reshape(N_HEADS, 1, HEAD_DIM))


def _attention(qkvh, c_rows, sb_gain, fox_gain, batch, seq, t_sb=512, tk_sb=256, t_fox=512, tk_fox=1024):
    assert t_sb % tk_sb == 0 and seq % t_sb == 0
    sb_scratch = [pltpu.VMEM((t_sb, 1), F32), pltpu.VMEM((t_sb, HEAD_DIM), F32)]
    o_sb = _attention_call(functools.partial(_sb_kernel, t=t_sb, tk=tk_sb), "sb_attention", qkvh, 0, (), (),
                           sb_gain, batch, seq, t_sb, sb_scratch)
    assert seq % tk_fox == 0 and tk_fox % t_fox == 0
    c_spec = pl.BlockSpec((None, None, 1, seq), lambda b, h, i: (b, h, 0, 0))
    fox_scratch = [pltpu.VMEM((seq, 2 * HEAD_DIM), BF16),
                   pltpu.VMEM((2, t_fox, tk_fox), F32),
                   pltpu.VMEM((t_fox, 1), F32),
                   pltpu.VMEM((t_fox, 2 * HEAD_DIM), F32)]
    o_fox = _attention_call(functools.partial(_fox_kernel, t=t_fox, tk=tk_fox), "fox_attention", qkvh, 3,
                            (c_rows,), (c_spec,), fox_gain, batch, seq, t_fox, fox_scratch)
    return o_sb, o_fox


def _outproj_kernel(a_ref, b_ref, w_ref, x_ref, o_ref):
    acc = jnp.dot(a_ref[...], w_ref[:GROUP_WIDTH, :], preferred_element_type=F32)
    acc = acc + jnp.dot(b_ref[...], w_ref[GROUP_WIDTH:, :], preferred_element_type=F32)
    o_ref[...] = x_ref[...] + acc


def _outproj(a, b, w_o, x, tm=1024, tn=1024):
    m = a.shape[0]
    n = w_o.shape[1]
    return pl.pallas_call(
        _outproj_kernel,
        grid=(m // tm, n // tn),
        in_specs=[pl.BlockSpec((tm, GROUP_WIDTH), lambda i, j: (i, 0)),
                  pl.BlockSpec((tm, GROUP_WIDTH), lambda i, j: (i, 0)),
                  pl.BlockSpec((2 * GROUP_WIDTH, tn), lambda i, j: (0, j)),
                  pl.BlockSpec((tm, tn), lambda i, j: (i, j))],
        out_specs=pl.BlockSpec((tm, tn), lambda i, j: (i, j)),
        out_shape=jax.ShapeDtypeStruct((m, n), F32),
        compiler_params=_params(2),
        name="out_proj",
    )(a, b, w_o, x)


def _mixer(x, norm_g, w_in, b_f, sb_g, fox_g, w_o, batch, seq):
    h = _rmsnorm(x, norm_g, BF16)
    qkvh = _inproj(h, w_in)
    b_row = jnp.pad(b_f, (0, LANES - N_HEADS)).reshape(1, LANES)
    c = _forget_cumsum(h, w_in, b_row, batch, seq)
    c_rows = c[:, :N_HEADS, :].reshape(batch, N_HEADS, 1, seq)
    o_sb, o_fox = _attention(qkvh, c_rows, sb_g, fox_g, batch, seq)
    return _outproj(o_sb, o_fox, w_o.astype(BF16), x)


@jax.jit
def kernel(x, norm_ffn1_g, ffn1_w_gate, ffn1_w_up, ffn1_w_down, norm_mix_g, w_in, b_f,
           sb_out_g, fox_out_g, w_o, norm_ffn2_g, ffn2_w_gate, ffn2_w_up, ffn2_w_down,
           norm_final_g):
    batch, seq, d = x.shape
    depth = w_in.shape[0]
    xf = x.reshape(batch * seq, d)
    for l in range(depth):
        xf = _ffn(xf, norm_ffn1_g[l], ffn1_w_gate[l], ffn1_w_up[l], ffn1_w_down[l])
        xf = _mixer(xf, norm_mix_g[l], w_in[l], b_f[l], sb_out_g[l], fox_out_g[l], w_o[l], batch, seq)
        xf = _ffn(xf, norm_ffn2_g[l], ffn2_w_gate[l], ffn2_w_up[l], ffn2_w_down[l])
    out = _rmsnorm(xf, norm_final_g, F32)
    return out.reshape(batch, seq, d)
```

```python
import functools
import math

import jax
import jax.numpy as jnp
from jax import lax
from jax.experimental import pallas as pl
from jax.experimental.pallas import tpu as pltpu

D_MODEL = 4096
HEAD_DIM = 128
N_HEADS = 16
GROUP_WIDTH = N_HEADS * HEAD_DIM
QKV_WIDTH = 6 * GROUP_WIDTH
EPS = 1e-6
FFN_RESIDUAL_SCALE = 0.5
ATTN_SCALE = HEAD_DIM ** -0.5
LOG2E = math.log2(math.e)
NEG_BIG = -1e30
F32_EXP2_UNDERFLOW = -152.0
FOX_SKIP_BELOW = -160.0
NORM_BOUND_MARGIN = 1.01

LANES = 128
VMEM_LIMIT_BYTES = 58 * 1024 * 1024

F32 = jnp.float32
BF16 = jnp.bfloat16


def _params(n_axes):
    return pltpu.CompilerParams(
        dimension_semantics=("arbitrary",) * n_axes,
        vmem_limit_bytes=VMEM_LIMIT_BYTES,
    )


def _rmsnorm_kernel(x_ref, g_ref, o_ref):
    x = x_ref[...]
    y = x * lax.rsqrt(jnp.mean(x * x, axis=-1, keepdims=True) + EPS)
    o_ref[...] = (y * g_ref[...]).astype(o_ref.dtype)


def _rmsnorm(x, g, out_dtype, tm=256):
    m, d = x.shape
    return pl.pallas_call(
        _rmsnorm_kernel,
        grid=(m // tm,),
        in_specs=[pl.BlockSpec((tm, d), lambda i: (i, 0)),
                  pl.BlockSpec((1, d), lambda i: (0, 0))],
        out_specs=pl.BlockSpec((tm, d), lambda i: (i, 0)),
        out_shape=jax.ShapeDtypeStruct((m, d), out_dtype),
        compiler_params=_params(1),
        name="rmsnorm",
    )(x, g.reshape(1, d))


def _gateup_kernel(h_ref, wg_ref, wu_ref, o_ref, wg_bf, wu_bf, *, tf, tail):
    j = pl.program_id(0)
    i = pl.program_id(1)
    nj = pl.num_programs(0)

    def step(width):
        @pl.when(i == 0)
        def _():
            wg_bf[:, :width] = wg_ref[:, :width].astype(BF16)
            wu_bf[:, :width] = wu_ref[:, :width].astype(BF16)

        h = h_ref[...]
        gate = jnp.dot(h, wg_bf[:, :width], preferred_element_type=F32)
        up = jnp.dot(h, wu_bf[:, :width], preferred_element_type=F32)
        o_ref[:, :width] = (gate * (1.0 / (1.0 + jnp.exp(-gate))) * up).astype(o_ref.dtype)

    if tail == tf:
        step(tf)
    else:
        pl.when(j < nj - 1)(lambda: step(tf))
        pl.when(j == nj - 1)(lambda: step(tail))


def _gateup(h, w_gate, w_up, tm=1024, tf=512):
    m, d = h.shape
    f = w_gate.shape[1]
    nj = pl.cdiv(f, tf)
    tail = f - (nj - 1) * tf
    w_spec = pl.BlockSpec((d, tf), lambda j, i: (0, j), pipeline_mode=pl.Buffered(1))
    return pl.pallas_call(
        functools.partial(_gateup_kernel, tf=tf, tail=tail),
        grid=(nj, m // tm),
        in_specs=[pl.BlockSpec((tm, d), lambda j, i: (i, 0)), w_spec, w_spec],
        out_specs=pl.BlockSpec((tm, tf), lambda j, i: (i, j)),
        out_shape=jax.ShapeDtypeStruct((m, f), BF16),
        scratch_shapes=[pltpu.VMEM((d, tf), BF16), pltpu.VMEM((d, tf), BF16)],
        compiler_params=_params(2),
        name="ffn_gateup",
    )(h, w_gate, w_up)


def _down_kernel(a_ref, w_ref, x_ref, o_ref, acc_ref, *, nk, tail):
    k = pl.program_id(2)

    @pl.when(k == 0)
    def _():
        acc_ref[...] = jnp.zeros_like(acc_ref)

    @pl.when(k < nk - 1)
    def _():
        acc_ref[...] += jnp.dot(a_ref[...], w_ref[...], preferred_element_type=F32)

    @pl.when(k == nk - 1)
    def _():
        last = jnp.dot(a_ref[:, :tail], w_ref[:tail, :], preferred_element_type=F32)
        o_ref[...] = x_ref[...] + FFN_RESIDUAL_SCALE * (acc_ref[...] + last)


def _down(a, w_d, x, tm=1024, tn=1024, tk=2816):
    m, f = a.shape
    n = w_d.shape[1]
    nk = pl.cdiv(f, tk)
    tail = f - (nk - 1) * tk
    return pl.pallas_call(
        functools.partial(_down_kernel, nk=nk, tail=tail),
        grid=(m // tm, n // tn, nk),
        in_specs=[pl.BlockSpec((tm, tk), lambda i, j, k: (i, k)),
                  pl.BlockSpec((tk, tn), lambda i, j, k: (k, j)),
                  pl.BlockSpec((tm, tn), lambda i, j, k: (i, j))],
        out_specs=pl.BlockSpec((tm, tn), lambda i, j, k: (i, j)),
        out_shape=jax.ShapeDtypeStruct((m, n), F32),
        scratch_shapes=[pltpu.VMEM((tm, tn), F32)],
        compiler_params=_params(3),
        name="ffn_down",
    )(a, w_d, x)


def _ffn(x, norm_g, w_gate, w_up, w_down):
    h = _rmsnorm(x, norm_g, BF16)
    a = _gateup(h, w_gate, w_up)
    return _down(a, w_down.astype(BF16), x)


def _inproj_kernel(h_ref, w_ref, o_ref, w_bf, *, heads_per_block):
    i = pl.program_id(1)

    @pl.when(i == 0)
    def _():
        w_bf[...] = w_ref[...].astype(BF16)

    r = jnp.dot(h_ref[...], w_bf[...], preferred_element_type=F32)
    for c in range(heads_per_block):
        o_ref[c] = r[:, c * HEAD_DIM:(c + 1) * HEAD_DIM].astype(o_ref.dtype)


def _inproj(h, w_in, tm=1024, tn=1024):
    m, d = h.shape
    hpb = tn // HEAD_DIM
    return pl.pallas_call(
        functools.partial(_inproj_kernel, heads_per_block=hpb),
        grid=(QKV_WIDTH // tn, m // tm),
        in_specs=[pl.BlockSpec((tm, d), lambda j, i: (i, 0)),
                  pl.BlockSpec((d, tn), lambda j, i: (0, j), pipeline_mode=pl.Buffered(1))],
        out_specs=pl.BlockSpec((hpb, tm, HEAD_DIM), lambda j, i: (j, i, 0)),
        out_shape=jax.ShapeDtypeStruct((QKV_WIDTH // HEAD_DIM, m, HEAD_DIM), BF16),
        scratch_shapes=[pltpu.VMEM((d, tn), BF16)],
        compiler_params=_params(2),
        name="in_proj",
    )(h, w_in)


def _split3(x):
    hi = x.astype(BF16)
    r = x - hi.astype(F32)
    mid = r.astype(BF16)
    lo = (r - mid.astype(F32)).astype(BF16)
    return hi, mid, lo


def _forget_kernel(h_ref, w_ref, b_ref, o_ref, w_bf, carry_ref, *, ts, n_valid):
    s = pl.program_id(1)
    valid = lax.broadcasted_iota(jnp.int32, (1, LANES), 1) < n_valid

    @pl.when(jnp.logical_and(pl.program_id(0) == 0, s == 0))
    def _():
        w_bf[...] = jnp.where(valid, w_ref[...], 0.0).astype(BF16)

    @pl.when(s == 0)
    def _():
        carry_ref[...] = jnp.zeros_like(carry_ref)

    logit = jnp.dot(h_ref[...], w_bf[...], preferred_element_type=F32) + b_ref[...]
    log_f = (jnp.minimum(logit, 0.0) - jnp.log1p(jnp.exp(-jnp.abs(logit)))).T
    r = lax.broadcasted_iota(jnp.int32, (ts, ts), 0)
    c = lax.broadcasted_iota(jnp.int32, (ts, ts), 1)
    tri = (r <= c).astype(BF16)
    hi, mid, lo = _split3(log_f)
    cum = (jnp.dot(hi, tri, preferred_element_type=F32)
           + jnp.dot(mid, tri, preferred_element_type=F32)
           + jnp.dot(lo, tri, preferred_element_type=F32)) + carry_ref[...]
    o_ref[...] = cum
    carry_ref[...] = cum[:, ts - 1:ts]


def _forget_cumsum(h, w_in, b_row, batch, seq, ts=512):
    d = h.shape[1]
    ns = seq // ts
    n_valid = w_in.shape[1] - QKV_WIDTH
    return pl.pallas_call(
        functools.partial(_forget_kernel, ts=ts, n_valid=n_valid),
        grid=(batch, ns),
        in_specs=[pl.BlockSpec((ts, d), lambda b, s: (b * ns + s, 0)),
                  pl.BlockSpec((d, LANES), lambda b, s: (0, QKV_WIDTH // LANES)),
                  pl.BlockSpec((1, LANES), lambda b, s: (0, 0))],
        out_specs=pl.BlockSpec((None, LANES, ts), lambda b, s: (b, 0, s)),
        out_shape=jax.ShapeDtypeStruct((batch, LANES, seq), F32),
        scratch_shapes=[pltpu.VMEM((d, LANES), BF16), pltpu.VMEM((LANES, 1), F32)],
        compiler_params=_params(2),
        name="forget_cumsum",
    )(h, w_in, b_row)


def _head_rmsnorm(o, gain):
    return o * lax.rsqrt(jnp.mean(o * o, axis=-1, keepdims=True) + EPS) * gain


def _dot_nt(a, b):
    return lax.dot_general(a, b, (((1,), (1,)), ((), ())), preferred_element_type=F32)


def _fox_kernel(q_ref, k_ref, v_ref, c_ref, g_ref, o_ref, vaug_ref, kn2_ref, s_ref, m_ref, acc_ref, *, t, tk):
    i = pl.program_id(2)

    @pl.when(i == 0)
    def _():
        vaug_ref[:, :HEAD_DIM] = v_ref[...]
        vaug_ref[:, HEAD_DIM:] = jnp.ones(v_ref.shape, BF16)
        kf = k_ref[...].astype(F32)
        kn2_ref[...] = jnp.max(jnp.sum(kf * kf, axis=-1, keepdims=True), axis=0, keepdims=True)

    q = q_ref[...]
    c_q0 = c_ref[:, pl.ds(pl.multiple_of(i * t, t), LANES)][:, 0:1]
    n_blocks = ((i + 1) * t + tk - 1) // tk
    last = n_blocks - 1

    qf = q.astype(F32)
    qn2 = jnp.max(jnp.sum(qf * qf, axis=-1, keepdims=True), axis=0, keepdims=True)
    spread = 2.0 * NORM_BOUND_MARGIN * ATTN_SCALE * jnp.sqrt(qn2 * kn2_ref[...])
    dead = ((c_q0 - c_ref[...]) + spread) * LOG2E < FOX_SKIP_BELOW
    first = jnp.minimum(jnp.sum(dead.astype(F32)).astype(jnp.int32) // tk, last)

    def scores(j):
        start = pl.multiple_of(j * tk, tk)
        bias = (c_q0 - c_ref[:, pl.ds(start, tk)]) * LOG2E
        return _dot_nt(q, k_ref[pl.ds(start, tk), :]) * (ATTN_SCALE * LOG2E) + bias

    def step(j, cur, masked):
        if not masked:
            s_ref[1 - cur] = scores(j + 1)
        s = s_ref[cur]
        if masked:
            row = i * t + lax.broadcasted_iota(jnp.int32, (t, tk), 0)
            col = j * tk + lax.broadcasted_iota(jnp.int32, (t, tk), 1)
            s = jnp.where(col <= row, s, NEG_BIG)
        m_old = m_ref[...]
        m_new = jnp.maximum(m_old, jnp.max(s, axis=-1, keepdims=True))
        p = jnp.exp2(s - m_new).astype(BF16)
        vs = vaug_ref[pl.ds(pl.multiple_of(j * tk, tk), tk), :]
        acc_ref[...] = jnp.exp2(m_old - m_new) * acc_ref[...] + jnp.dot(p, vs, preferred_element_type=F32)
        m_ref[...] = m_new

    def either_half(j, masked):
        pl.when(j % 2 == 0)(lambda: step(j, 0, masked))
        pl.when(j % 2 == 1)(lambda: step(j, 1, masked))

    m_ref[...] = jnp.full(m_ref.shape, NEG_BIG, F32)
    acc_ref[...] = jnp.zeros(acc_ref.shape, F32)

    def stage_first(half):
        s_ref[half] = scores(first)

    pl.when(first % 2 == 0)(lambda: stage_first(0))
    pl.when(first % 2 == 1)(lambda: stage_first(1))

    @pl.loop(first, last)
    def _(j):
        either_half(j, False)

    either_half(last, True)
    acc = acc_ref[...]
    o_ref[...] = _head_rmsnorm(acc[:, :HEAD_DIM] / acc[:, HEAD_DIM:], g_ref[...]).astype(o_ref.dtype)


def _sb_kernel(q_ref, k_ref, v_ref, g_ref, o_ref, rem_ref, acc_ref, *, t, tk):
    i = pl.program_id(2)
    q = q_ref[...]
    tri = (lax.broadcasted_iota(jnp.int32, (tk, tk), 0)
           > lax.broadcasted_iota(jnp.int32, (tk, tk), 1)).astype(BF16)
    tri2 = jnp.concatenate([tri, tri], axis=0)
    rem_ref[...] = jnp.zeros(rem_ref.shape, F32)
    acc_ref[...] = jnp.zeros(acc_ref.shape, F32)

    def block(j, masked):
        start = pl.multiple_of(j * tk, tk)
        z = _dot_nt(q, k_ref[pl.ds(start, tk), :]) * (ATTN_SCALE * LOG2E)
        ls_pos = jnp.minimum(z, 0.0) - jnp.log2(1.0 + jnp.exp2(-jnp.abs(z)))
        log_keep = ls_pos - z
        if masked:
            past = (j * tk + lax.broadcasted_iota(jnp.int32, (t, tk), 1)
                    < i * t + lax.broadcasted_iota(jnp.int32, (t, tk), 0))
            log_keep = jnp.where(past, log_keep, 0.0)
        hi = log_keep.astype(BF16)
        lo = (log_keep - hi.astype(F32)).astype(BF16)
        after = jnp.dot(jnp.concatenate([hi, lo], axis=1), tri2, preferred_element_type=F32)
        rem = rem_ref[...]
        a = jnp.exp2(ls_pos + after + rem)
        if masked:
            a = jnp.where(past, a, 0.0)
        acc_ref[...] += jnp.dot(a.astype(BF16), v_ref[pl.ds(start, tk), :], preferred_element_type=F32)
        rem = rem + jnp.sum(log_keep, axis=-1, keepdims=True)
        rem_ref[...] = rem
        return jnp.max(rem)

    n_diag = t // tk
    first = (i + 1) * n_diag - 1
    for d in range(n_diag):
        rem_max = block(first - d, True)

    def cond(state):
        j, rem_max = state
        return jnp.logical_and(j >= 0, rem_max > F32_EXP2_UNDERFLOW)

    def body(state):
        j, _ = state
        return j - 1, block(j, False)

    lax.while_loop(cond, body, (first - n_diag, rem_max))
    o_ref[...] = _head_rmsnorm(acc_ref[...], g_ref[...]).astype(o_ref.dtype)


def _attention_call(kernel_fn, name, qkvh, first_section, extra_inputs, extra_specs, gain, batch, seq, t,
                    scratch_shapes=()):
    nq = seq // t

    def section_spec(section, rows):
        blocks_per_batch = seq // rows
        return pl.BlockSpec(
            (None, rows, HEAD_DIM),
            lambda b, h, i: (section * N_HEADS + h, b * blocks_per_batch + (i if rows == t else 0), 0))

    return pl.pallas_call(
        kernel_fn,
        grid=(batch, N_HEADS, nq),
        in_specs=[section_spec(first_section, t), section_spec(first_section + 1, seq),
                  section_spec(first_section + 2, seq), *extra_specs,
                  pl.BlockSpec((None, 1, HEAD_DIM), lambda b, h, i: (h, 0, 0))],
        out_specs=pl.BlockSpec((t, HEAD_DIM), lambda b, h, i: (b * nq + i, h)),
        out_shape=jax.ShapeDtypeStruct((batch * seq, GROUP_WIDTH), BF16),
        scratch_shapes=list(scratch_shapes),
        compiler_params=_params(3),
        name=name,
    )(qkvh, qkvh, qkvh, *extra_inputs, gain.reshape(N_HEADS, 1, HEAD_DIM))


def _attention(qkvh, c_rows, sb_gain, fox_gain, batch, seq, t_sb=512, tk_sb=256, t_fox=512, tk_fox=1024):
    assert t_sb % tk_sb == 0 and seq % t_sb == 0
    sb_scratch = [pltpu.VMEM((t_sb, 1), F32), pltpu.VMEM((t_sb, HEAD_DIM), F32)]
    o_sb = _attention_call(functools.partial(_sb_kernel, t=t_sb, tk=tk_sb), "sb_attention", qkvh, 0, (), (),
                           sb_gain, batch, seq, t_sb, sb_scratch)
    assert seq % tk_fox == 0 and tk_fox % t_fox == 0
    c_spec = pl.BlockSpec((None, None, 1, seq), lambda b, h, i: (b, h, 0, 0))
    fox_scratch = [pltpu.VMEM((seq, 2 * HEAD_DIM), BF16),
                   pltpu.VMEM((1, 1), F32),
                   pltpu.VMEM((2, t_fox, tk_fox), F32),
                   pltpu.VMEM((t_fox, 1), F32),
                   pltpu.VMEM((t_fox, 2 * HEAD_DIM), F32)]
    o_fox = _attention_call(functools.partial(_fox_kernel, t=t_fox, tk=tk_fox), "fox_attention", qkvh, 3,
                            (c_rows,), (c_spec,), fox_gain, batch, seq, t_fox, fox_scratch)
    return o_sb, o_fox


def _outproj_kernel(a_ref, b_ref, w_ref, x_ref, o_ref):
    acc = jnp.dot(a_ref[...], w_ref[:GROUP_WIDTH, :], preferred_element_type=F32)
    acc = acc + jnp.dot(b_ref[...], w_ref[GROUP_WIDTH:, :], preferred_element_type=F32)
    o_ref[...] = x_ref[...] + acc


def _outproj(a, b, w_o, x, tm=1024, tn=1024):
    m = a.shape[0]
    n = w_o.shape[1]
    return pl.pallas_call(
        _outproj_kernel,
        grid=(m // tm, n // tn),
        in_specs=[pl.BlockSpec((tm, GROUP_WIDTH), lambda i, j: (i, 0)),
                  pl.BlockSpec((tm, GROUP_WIDTH), lambda i, j: (i, 0)),
                  pl.BlockSpec((2 * GROUP_WIDTH, tn), lambda i, j: (0, j)),
                  pl.BlockSpec((tm, tn), lambda i, j: (i, j))],
        out_specs=pl.BlockSpec((tm, tn), lambda i, j: (i, j)),
        out_shape=jax.ShapeDtypeStruct((m, n), F32),
        compiler_params=_params(2),
        name="out_proj",
    )(a, b, w_o, x)


def _mixer(x, norm_g, w_in, b_f, sb_g, fox_g, w_o, batch, seq):
    h = _rmsnorm(x, norm_g, BF16)
    qkvh = _inproj(h, w_in)
    b_row = jnp.pad(b_f, (0, LANES - N_HEADS)).reshape(1, LANES)
    c = _forget_cumsum(h, w_in, b_row, batch, seq)
    c_rows = c[:, :N_HEADS, :].reshape(batch, N_HEADS, 1, seq)
    o_sb, o_fox = _attention(qkvh, c_rows, sb_g, fox_g, batch, seq)
    return _outproj(o_sb, o_fox, w_o.astype(BF16), x)


@jax.jit
def kernel(x, norm_ffn1_g, ffn1_w_gate, ffn1_w_up, ffn1_w_down, norm_mix_g, w_in, b_f,
           sb_out_g, fox_out_g, w_o, norm_ffn2_g, ffn2_w_gate, ffn2_w_up, ffn2_w_down,
           norm_final_g):
    batch, seq, d = x.shape
    depth = w_in.shape[0]
    xf = x.reshape(batch * seq, d)
    for l in range(depth):
        xf = _ffn(xf, norm_ffn1_g[l], ffn1_w_gate[l], ffn1_w_up[l], ffn1_w_down[l])
        xf = _mixer(xf, norm_mix_g[l], w_in[l], b_f[l], sb_out_g[l], fox_out_g[l], w_o[l], batch, seq)
        xf = _ffn(xf, norm_ffn2_g[l], ffn2_w_gate[l], ffn2_w_up[l], ffn2_w_down[l])
    out = _rmsnorm(xf, norm_final_g, F32)
    return out.reshape(batch, seq, d)
```

```python
import functools
import math

import jax
import jax.numpy as jnp
from jax import lax
from jax.experimental import pallas as pl
from jax.experimental.pallas import tpu as pltpu

D_MODEL = 4096
HEAD_DIM = 128
N_HEADS = 16
GROUP_WIDTH = N_HEADS * HEAD_DIM
QKV_WIDTH = 6 * GROUP_WIDTH
EPS = 1e-6
FFN_RESIDUAL_SCALE = 0.5
ATTN_SCALE = HEAD_DIM ** -0.5
LOG2E = math.log2(math.e)
NEG_BIG = -1e30
F32_EXP2_UNDERFLOW = -152.0
FOX_SKIP_BELOW = -160.0
NORM_BOUND_MARGIN = 1.01

LANES = 128
VMEM_LIMIT_BYTES = 58 * 1024 * 1024

F32 = jnp.float32
BF16 = jnp.bfloat16


def _params(n_axes):
    return pltpu.CompilerParams(
        dimension_semantics=("arbitrary",) * n_axes,
        vmem_limit_bytes=VMEM_LIMIT_BYTES,
    )


def _rmsnorm_kernel(x_ref, g_ref, o_ref):
    x = x_ref[...]
    y = x * lax.rsqrt(jnp.mean(x * x, axis=-1, keepdims=True) + EPS)
    o_ref[...] = (y * g_ref[...]).astype(o_ref.dtype)


def _rmsnorm(x, g, out_dtype, tm=256):
    m, d = x.shape
    return pl.pallas_call(
        _rmsnorm_kernel,
        grid=(m // tm,),
        in_specs=[pl.BlockSpec((tm, d), lambda i: (i, 0)),
                  pl.BlockSpec((1, d), lambda i: (0, 0))],
        out_specs=pl.BlockSpec((tm, d), lambda i: (i, 0)),
        out_shape=jax.ShapeDtypeStruct((m, d), out_dtype),
        compiler_params=_params(1),
        name="rmsnorm",
    )(x, g.reshape(1, d))


def _gateup_kernel(h_ref, wg_ref, wu_ref, o_ref, wg_bf, wu_bf, *, tf, tail):
    j = pl.program_id(0)
    i = pl.program_id(1)
    nj = pl.num_programs(0)

    def step(width):
        @pl.when(i == 0)
        def _():
            wg_bf[:, :width] = wg_ref[:, :width].astype(BF16)
            wu_bf[:, :width] = wu_ref[:, :width].astype(BF16)

        h = h_ref[...]
        gate = jnp.dot(h, wg_bf[:, :width], preferred_element_type=F32)
        up = jnp.dot(h, wu_bf[:, :width], preferred_element_type=F32)
        o_ref[:, :width] = (gate * (1.0 / (1.0 + jnp.exp(-gate))) * up).astype(o_ref.dtype)

    if tail == tf:
        step(tf)
    else:
        pl.when(j < nj - 1)(lambda: step(tf))
        pl.when(j == nj - 1)(lambda: step(tail))


def _gateup(h, w_gate, w_up, tm=1024, tf=512):
    m, d = h.shape
    f = w_gate.shape[1]
    nj = pl.cdiv(f, tf)
    tail = f - (nj - 1) * tf
    w_spec = pl.BlockSpec((d, tf), lambda j, i: (0, j), pipeline_mode=pl.Buffered(1))
    return pl.pallas_call(
        functools.partial(_gateup_kernel, tf=tf, tail=tail),
        grid=(nj, m // tm),
        in_specs=[pl.BlockSpec((tm, d), lambda j, i: (i, 0)), w_spec, w_spec],
        out_specs=pl.BlockSpec((tm, tf), lambda j, i: (i, j)),
        out_shape=jax.ShapeDtypeStruct((m, f), BF16),
        scratch_shapes=[pltpu.VMEM((d, tf), BF16), pltpu.VMEM((d, tf), BF16)],
        compiler_params=_params(2),
        name="ffn_gateup",
    )(h, w_gate, w_up)


def _down_kernel(a_ref, w_ref, x_ref, o_ref, acc_ref, *, nk, tail):
    k = pl.program_id(2)

    @pl.when(k == 0)
    def _():
        acc_ref[...] = jnp.zeros_like(acc_ref)

    @pl.when(k < nk - 1)
    def _():
        acc_ref[...] += jnp.dot(a_ref[...], w_ref[...], preferred_element_type=F32)

    @pl.when(k == nk - 1)
    def _():
        last = jnp.dot(a_ref[:, :tail], w_ref[:tail, :], preferred_element_type=F32)
        o_ref[...] = x_ref[...] + FFN_RESIDUAL_SCALE * (acc_ref[...] + last)


def _down(a, w_d, x, tm=1024, tn=1024, tk=2816):
    m, f = a.shape
    n = w_d.shape[1]
    nk = pl.cdiv(f, tk)
    tail = f - (nk - 1) * tk
    return pl.pallas_call(
        functools.partial(_down_kernel, nk=nk, tail=tail),
        grid=(m // tm, n // tn, nk),
        in_specs=[pl.BlockSpec((tm, tk), lambda i, j, k: (i, k)),
                  pl.BlockSpec((tk, tn), lambda i, j, k: (k, j)),
                  pl.BlockSpec((tm, tn), lambda i, j, k: (i, j))],
        out_specs=pl.BlockSpec((tm, tn), lambda i, j, k: (i, j)),
        out_shape=jax.ShapeDtypeStruct((m, n), F32),
        scratch_shapes=[pltpu.VMEM((tm, tn), F32)],
        compiler_params=_params(3),
        name="ffn_down",
    )(a, w_d, x)


def _ffn(x, norm_g, w_gate, w_up, w_down):
    h = _rmsnorm(x, norm_g, BF16)
    a = _gateup(h, w_gate, w_up)
    return _down(a, w_down.astype(BF16), x)


def _inproj_kernel(h_ref, wt_ref, o_ref, wt_bf, *, heads_per_block):
    i = pl.program_id(1)

    @pl.when(i == 0)
    def _():
        wt_bf[...] = wt_ref[...].astype(BF16)

    r = _dot_nt(h_ref[...], wt_bf[...])
    for c in range(heads_per_block):
        o_ref[c] = r[:, c * HEAD_DIM:(c + 1) * HEAD_DIM].astype(o_ref.dtype)


def _inproj(h, w_in_t, tm=1024, tn=1024):
    m, d = h.shape
    hpb = tn // HEAD_DIM
    return pl.pallas_call(
        functools.partial(_inproj_kernel, heads_per_block=hpb),
        grid=(QKV_WIDTH // tn, m // tm),
        in_specs=[pl.BlockSpec((tm, d), lambda j, i: (i, 0)),
                  pl.BlockSpec((tn, d), lambda j, i: (j, 0), pipeline_mode=pl.Buffered(1))],
        out_specs=pl.BlockSpec((hpb, tm, HEAD_DIM), lambda j, i: (j, i, 0)),
        out_shape=jax.ShapeDtypeStruct((QKV_WIDTH // HEAD_DIM, m, HEAD_DIM), BF16),
        scratch_shapes=[pltpu.VMEM((tn, d), BF16)],
        compiler_params=_params(2),
        name="in_proj",
    )(h, w_in_t)


def _split3(x):
    hi = x.astype(BF16)
    r = x - hi.astype(F32)
    mid = r.astype(BF16)
    lo = (r - mid.astype(F32)).astype(BF16)
    return hi, mid, lo


def _forget_kernel(h_ref, wt_ref, b_ref, o_ref, wt_bf, carry_ref, *, ts, n_valid):
    s = pl.program_id(1)
    valid = lax.broadcasted_iota(jnp.int32, (LANES, 1), 0) < n_valid

    @pl.when(jnp.logical_and(pl.program_id(0) == 0, s == 0))
    def _():
        wt_bf[...] = jnp.where(valid, wt_ref[...], 0.0).astype(BF16)

    @pl.when(s == 0)
    def _():
        carry_ref[...] = jnp.zeros_like(carry_ref)

    logit = _dot_nt(wt_bf[...], h_ref[...]) + b_ref[...]
    log_f = jnp.minimum(logit, 0.0) - jnp.log1p(jnp.exp(-jnp.abs(logit)))
    r = lax.broadcasted_iota(jnp.int32, (ts, ts), 0)
    c = lax.broadcasted_iota(jnp.int32, (ts, ts), 1)
    tri = (r <= c).astype(BF16)
    hi, mid, lo = _split3(log_f)
    cum = (jnp.dot(hi, tri, preferred_element_type=F32)
           + jnp.dot(mid, tri, preferred_element_type=F32)
           + jnp.dot(lo, tri, preferred_element_type=F32)) + carry_ref[...]
    o_ref[...] = cum
    carry_ref[...] = cum[:, ts - 1:ts]


def _forget_cumsum(h, w_in_t, b_col, batch, seq, ts=512):
    d = h.shape[1]
    ns = seq // ts
    n_valid = w_in_t.shape[0] - QKV_WIDTH
    return pl.pallas_call(
        functools.partial(_forget_kernel, ts=ts, n_valid=n_valid),
        grid=(batch, ns),
        in_specs=[pl.BlockSpec((ts, d), lambda b, s: (b * ns + s, 0)),
                  pl.BlockSpec((LANES, d), lambda b, s: (QKV_WIDTH // LANES, 0)),
                  pl.BlockSpec((LANES, 1), lambda b, s: (0, 0))],
        out_specs=pl.BlockSpec((None, LANES, ts), lambda b, s: (b, 0, s)),
        out_shape=jax.ShapeDtypeStruct((batch, LANES, seq), F32),
        scratch_shapes=[pltpu.VMEM((LANES, d), BF16), pltpu.VMEM((LANES, 1), F32)],
        compiler_params=_params(2),
        name="forget_cumsum",
    )(h, w_in_t, b_col)


def _head_rmsnorm(o, gain):
    return o * lax.rsqrt(jnp.mean(o * o, axis=-1, keepdims=True) + EPS) * gain


def _dot_nt(a, b):
    return lax.dot_general(a, b, (((1,), (1,)), ((), ())), preferred_element_type=F32)


def _fox_kernel(q_ref, k_ref, v_ref, c_ref, g_ref, o_ref, vaug_ref, kn2_ref, s_ref, m_ref, acc_ref, *, t, tk):
    i = pl.program_id(2)

    @pl.when(i == 0)
    def _():
        vaug_ref[:, :HEAD_DIM] = v_ref[...]
        vaug_ref[:, HEAD_DIM:] = jnp.ones(v_ref.shape, BF16)
        kf = k_ref[...].astype(F32)
        kn2_ref[...] = jnp.max(jnp.sum(kf * kf, axis=-1, keepdims=True), axis=0, keepdims=True)

    q = q_ref[...]
    c_q0 = c_ref[:, pl.ds(pl.multiple_of(i * t, t), LANES)][:, 0:1]
    n_blocks = ((i + 1) * t + tk - 1) // tk
    last = n_blocks - 1

    qf = q.astype(F32)
    qn2 = jnp.max(jnp.sum(qf * qf, axis=-1, keepdims=True), axis=0, keepdims=True)
    spread = 2.0 * NORM_BOUND_MARGIN * ATTN_SCALE * jnp.sqrt(qn2 * kn2_ref[...])
    dead = ((c_q0 - c_ref[...]) + spread) * LOG2E < FOX_SKIP_BELOW
    first = jnp.minimum(jnp.sum(dead.astype(F32)).astype(jnp.int32) // tk, last)

    def scores(j):
        start = pl.multiple_of(j * tk, tk)
        bias = (c_q0 - c_ref[:, pl.ds(start, tk)]) * LOG2E
        return _dot_nt(q, k_ref[pl.ds(start, tk), :]) * (ATTN_SCALE * LOG2E) + bias

    def step(j, cur, masked):
        if not masked:
            s_ref[1 - cur] = scores(j + 1)
        s = s_ref[cur]
        if masked:
            row = i * t + lax.broadcasted_iota(jnp.int32, (t, tk), 0)
            col = j * tk + lax.broadcasted_iota(jnp.int32, (t, tk), 1)
            s = jnp.where(col <= row, s, NEG_BIG)
        m_old = m_ref[...]
        m_new = jnp.maximum(m_old, jnp.max(s, axis=-1, keepdims=True))
        p = jnp.exp2(s - m_new).astype(BF16)
        vs = vaug_ref[pl.ds(pl.multiple_of(j * tk, tk), tk), :]
        acc_ref[...] = jnp.exp2(m_old - m_new) * acc_ref[...] + jnp.dot(p, vs, preferred_element_type=F32)
        m_ref[...] = m_new

    def either_half(j, masked):
        pl.when(j % 2 == 0)(lambda: step(j, 0, masked))
        pl.when(j % 2 == 1)(lambda: step(j, 1, masked))

    m_ref[...] = jnp.full(m_ref.shape, NEG_BIG, F32)
    acc_ref[...] = jnp.zeros(acc_ref.shape, F32)

    def stage_first(half):
        s_ref[half] = scores(first)

    pl.when(first % 2 == 0)(lambda: stage_first(0))
    pl.when(first % 2 == 1)(lambda: stage_first(1))

    @pl.loop(first, last)
    def _(j):
        either_half(j, False)

    either_half(last, True)
    acc = acc_ref[...]
    o_ref[...] = _head_rmsnorm(acc[:, :HEAD_DIM] / acc[:, HEAD_DIM:], g_ref[...]).astype(o_ref.dtype)


def _sb_kernel(q_ref, k_ref, v_ref, g_ref, o_ref, rem_ref, acc_ref, *, t, tk, heads):
    i = pl.program_id(2)
    tri = (lax.broadcasted_iota(jnp.int32, (tk, tk), 0)
           > lax.broadcasted_iota(jnp.int32, (tk, tk), 1)).astype(BF16)
    tri2 = jnp.concatenate([tri, tri], axis=0)

    def block(h, j, r0, rem, keep_fn):
        start = pl.multiple_of(j * tk, tk)
        z = _dot_nt(q_ref[h, r0:, :], k_ref[h, pl.ds(start, tk), :]) * (ATTN_SCALE * LOG2E)
        ls_pos = jnp.minimum(z, 0.0) - jnp.log2(1.0 + jnp.exp2(-jnp.abs(z)))
        log_keep = ls_pos - z
        if keep_fn is not None:
            keep = keep_fn(j, r0)
            log_keep = jnp.where(keep, log_keep, 0.0)
        hi = log_keep.astype(BF16)
        lo = (log_keep - hi.astype(F32)).astype(BF16)
        after = jnp.dot(jnp.concatenate([hi, lo], axis=1), tri2, preferred_element_type=F32)
        a = jnp.exp2(ls_pos + after + rem)
        if keep_fn is not None:
            a = jnp.where(keep, a, 0.0)
        out = jnp.dot(a.astype(BF16), v_ref[h, pl.ds(start, tk), :], preferred_element_type=F32)
        return out, jnp.sum(log_keep, axis=-1, keepdims=True)

    def causal(j, r0):
        return (j * tk + lax.broadcasted_iota(jnp.int32, (t - r0, tk), 1)
                < i * t + r0 + lax.broadcasted_iota(jnp.int32, (t - r0, tk), 0))

    def below(r0, x):
        return jnp.concatenate([jnp.zeros((r0, x.shape[1]), F32), x], axis=0) if r0 else x

    n_diag = t // tk
    first = (i + 1) * n_diag - 1
    extra = first - n_diag
    rem_max = []
    for h in range(heads):
        rem = jnp.zeros((t, 1), F32)
        acc = jnp.zeros((t, HEAD_DIM), F32)
        for d in range(n_diag):
            r0 = t - (d + 1) * tk
            out, tot = block(h, first - d, r0, rem[r0:], causal)
            acc = acc + below(r0, out)
            rem = rem + below(r0, tot)
        out, tot = block(h, jnp.maximum(extra, 0), 0, rem, lambda j, r0: extra >= 0)
        acc_ref[h] = acc + out
        rem = rem + tot
        rem_ref[h] = rem
        rem_max.append(jnp.max(rem))

    def cond(state):
        j, rem_max = state
        return jnp.logical_and(j >= 0, rem_max > F32_EXP2_UNDERFLOW)

    for h in range(heads):
        def body(state, h=h):
            j, _ = state
            rem = rem_ref[h]
            out, tot = block(h, j, 0, rem, None)
            acc_ref[h] += out
            rem_ref[h] = rem + tot
            return j - 1, jnp.max(rem + tot)

        lax.while_loop(cond, body, (extra - 1, rem_max[h]))
        o_ref[:, h * HEAD_DIM:(h + 1) * HEAD_DIM] = _head_rmsnorm(acc_ref[h], g_ref[h]).astype(o_ref.dtype)


def _attention_call(kernel_fn, name, qkvh, first_section, extra_inputs, extra_specs, gain, batch, seq, t,
                    scratch_shapes=(), heads=None):
    nq = seq // t
    per_step = heads or 1
    groups = N_HEADS // per_step

    def section_spec(section, rows):
        blocks_per_batch = seq // rows
        return pl.BlockSpec(
            (heads, rows, HEAD_DIM),
            lambda b, g, i: (section * groups + g, b * blocks_per_batch + (i if rows == t else 0), 0))

    return pl.pallas_call(
        kernel_fn,
        grid=(batch, groups, nq),
        in_specs=[section_spec(first_section, t), section_spec(first_section + 1, seq),
                  section_spec(first_section + 2, seq), *extra_specs,
                  pl.BlockSpec((heads, 1, HEAD_DIM), lambda b, g, i: (g, 0, 0))],
        out_specs=pl.BlockSpec((t, per_step * HEAD_DIM), lambda b, g, i: (b * nq + i, g)),
        out_shape=jax.ShapeDtypeStruct((batch * seq, GROUP_WIDTH), BF16),
        scratch_shapes=list(scratch_shapes),
        compiler_params=_params(3),
        name=name,
    )(qkvh, qkvh, qkvh, *extra_inputs, gain.reshape(N_HEADS, 1, HEAD_DIM))


def _attention(qkvh, c_rows, sb_gain, fox_gain, batch, seq, t_sb=512, tk_sb=256, heads_sb=2,
               t_fox=512, tk_fox=1024):
    assert t_sb % tk_sb == 0 and seq % t_sb == 0 and N_HEADS % heads_sb == 0
    sb_scratch = [pltpu.VMEM((heads_sb, t_sb, 1), F32), pltpu.VMEM((heads_sb, t_sb, HEAD_DIM), F32)]
    o_sb = _attention_call(functools.partial(_sb_kernel, t=t_sb, tk=tk_sb, heads=heads_sb), "sb_attention",
                           qkvh, 0, (), (), sb_gain, batch, seq, t_sb, sb_scratch, heads=heads_sb)
    assert seq % tk_fox == 0 and tk_fox % t_fox == 0
    c_spec = pl.BlockSpec((None, None, 1, seq), lambda b, h, i: (b, h, 0, 0))
    fox_scratch = [pltpu.VMEM((seq, 2 * HEAD_DIM), BF16),
                   pltpu.VMEM((1, 1), F32),
                   pltpu.VMEM((2, t_fox, tk_fox), F32),
                   pltpu.VMEM((t_fox, 1), F32),
                   pltpu.VMEM((t_fox, 2 * HEAD_DIM), F32)]
    o_fox = _attention_call(functools.partial(_fox_kernel, t=t_fox, tk=tk_fox), "fox_attention", qkvh, 3,
                            (c_rows,), (c_spec,), fox_gain, batch, seq, t_fox, fox_scratch)
    return o_sb, o_fox


def _outproj_kernel(a_ref, b_ref, w_ref, x_ref, o_ref):
    acc = jnp.dot(a_ref[...], w_ref[:GROUP_WIDTH, :], preferred_element_type=F32)
    acc = acc + jnp.dot(b_ref[...], w_ref[GROUP_WIDTH:, :], preferred_element_type=F32)
    o_ref[...] = x_ref[...] + acc


def _outproj(a, b, w_o, x, tm=1024, tn=1024):
    m = a.shape[0]
    n = w_o.shape[1]
    return pl.pallas_call(
        _outproj_kernel,
        grid=(m // tm, n // tn),
        in_specs=[pl.BlockSpec((tm, GROUP_WIDTH), lambda i, j: (i, 0)),
                  pl.BlockSpec((tm, GROUP_WIDTH), lambda i, j: (i, 0)),
                  pl.BlockSpec((2 * GROUP_WIDTH, tn), lambda i, j: (0, j)),
                  pl.BlockSpec((tm, tn), lambda i, j: (i, j))],
        out_specs=pl.BlockSpec((tm, tn), lambda i, j: (i, j)),
        out_shape=jax.ShapeDtypeStruct((m, n), F32),
        compiler_params=_params(2),
        name="out_proj",
    )(a, b, w_o, x)


def _mixer(x, norm_g, w_in, b_f, sb_g, fox_g, w_o, batch, seq):
    h = _rmsnorm(x, norm_g, BF16)
    w_in_t = w_in.T
    qkvh = _inproj(h, w_in_t)
    b_col = jnp.pad(b_f, (0, LANES - N_HEADS)).reshape(LANES, 1)
    c = _forget_cumsum(h, w_in_t, b_col, batch, seq)
    c_rows = c[:, :N_HEADS, :].reshape(batch, N_HEADS, 1, seq)
    o_sb, o_fox = _attention(qkvh, c_rows, sb_g, fox_g, batch, seq)
    return _outproj(o_sb, o_fox, w_o.astype(BF16), x)


@jax.jit
def kernel(x, norm_ffn1_g, ffn1_w_gate, ffn1_w_up, ffn1_w_down, norm_mix_g, w_in, b_f,
           sb_out_g, fox_out_g, w_o, norm_ffn2_g, ffn2_w_gate, ffn2_w_up, ffn2_w_down,
           norm_final_g):
    batch, seq, d = x.shape
    depth = w_in.shape[0]
    xf = x.reshape(batch * seq, d)
    for l in range(depth):
        xf = _ffn(xf, norm_ffn1_g[l], ffn1_w_gate[l], ffn1_w_up[l], ffn1_w_down[l])
        xf = _mixer(xf, norm_mix_g[l], w_in[l], b_f[l], sb_out_g[l], fox_out_g[l], w_o[l], batch, seq)
        xf = _ffn(xf, norm_ffn2_g[l], ffn2_w_gate[l], ffn2_w_up[l], ffn2_w_down[l])
    out = _rmsnorm(xf, norm_final_g, F32)
    return out.reshape(batch, seq, d)
```

```python
import functools
import math

import jax
import jax.numpy as jnp
from jax import lax
from jax.experimental import pallas as pl
from jax.experimental.pallas import tpu as pltpu

D_MODEL = 4096
HEAD_DIM = 128
N_HEADS = 16
GROUP_WIDTH = N_HEADS * HEAD_DIM
QKV_WIDTH = 6 * GROUP_WIDTH
EPS = 1e-6
FFN_RESIDUAL_SCALE = 0.5
ATTN_SCALE = HEAD_DIM ** -0.5
LOG2E = math.log2(math.e)
NEG_BIG = -1e30
F32_EXP2_UNDERFLOW = -152.0
FOX_SKIP_BELOW = -160.0
NORM_BOUND_MARGIN = 1.01

LANES = 128
VMEM_LIMIT_BYTES = 58 * 1024 * 1024

F32 = jnp.float32
BF16 = jnp.bfloat16


def _params(n_axes):
    return pltpu.CompilerParams(
        dimension_semantics=("arbitrary",) * n_axes,
        vmem_limit_bytes=VMEM_LIMIT_BYTES,
    )


def _rmsnorm_kernel(x_ref, g_ref, o_ref):
    x = x_ref[...]
    y = x * lax.rsqrt(jnp.mean(x * x, axis=-1, keepdims=True) + EPS)
    o_ref[...] = (y * g_ref[...]).astype(o_ref.dtype)


def _rmsnorm(x, g, out_dtype, tm=256):
    m, d = x.shape
    return pl.pallas_call(
        _rmsnorm_kernel,
        grid=(m // tm,),
        in_specs=[pl.BlockSpec((tm, d), lambda i: (i, 0)),
                  pl.BlockSpec((1, d), lambda i: (0, 0))],
        out_specs=pl.BlockSpec((tm, d), lambda i: (i, 0)),
        out_shape=jax.ShapeDtypeStruct((m, d), out_dtype),
        compiler_params=_params(1),
        name="rmsnorm",
    )(x, g.reshape(1, d))


def _gateup_kernel(h_ref, wg_ref, wu_ref, o_ref, wg_bf, wu_bf, *, tf, tail):
    j = pl.program_id(0)
    i = pl.program_id(1)
    nj = pl.num_programs(0)

    def step(width):
        @pl.when(i == 0)
        def _():
            wg_bf[:, :width] = wg_ref[:, :width].astype(BF16)
            wu_bf[:, :width] = wu_ref[:, :width].astype(BF16)

        h = h_ref[...]
        gate = jnp.dot(h, wg_bf[:, :width], preferred_element_type=F32)
        up = jnp.dot(h, wu_bf[:, :width], preferred_element_type=F32)
        o_ref[:, :width] = (gate * (1.0 / (1.0 + jnp.exp(-gate))) * up).astype(o_ref.dtype)

    if tail == tf:
        step(tf)
    else:
        pl.when(j < nj - 1)(lambda: step(tf))
        pl.when(j == nj - 1)(lambda: step(tail))


def _gateup(h, w_gate, w_up, tm=1024, tf=512):
    m, d = h.shape
    f = w_gate.shape[1]
    nj = pl.cdiv(f, tf)
    tail = f - (nj - 1) * tf
    w_spec = pl.BlockSpec((d, tf), lambda j, i: (0, j), pipeline_mode=pl.Buffered(1))
    return pl.pallas_call(
        functools.partial(_gateup_kernel, tf=tf, tail=tail),
        grid=(nj, m // tm),
        in_specs=[pl.BlockSpec((tm, d), lambda j, i: (i, 0)), w_spec, w_spec],
        out_specs=pl.BlockSpec((tm, tf), lambda j, i: (i, j)),
        out_shape=jax.ShapeDtypeStruct((m, f), BF16),
        scratch_shapes=[pltpu.VMEM((d, tf), BF16), pltpu.VMEM((d, tf), BF16)],
        compiler_params=_params(2),
        name="ffn_gateup",
    )(h, w_gate, w_up)


def _down_kernel(a_ref, w_ref, x_ref, o_ref, acc_ref, *, nk, tail):
    k = pl.program_id(2)

    @pl.when(k == 0)
    def _():
        acc_ref[...] = jnp.zeros_like(acc_ref)

    @pl.when(k < nk - 1)
    def _():
        acc_ref[...] += jnp.dot(a_ref[...], w_ref[...], preferred_element_type=F32)

    @pl.when(k == nk - 1)
    def _():
        last = jnp.dot(a_ref[:, :tail], w_ref[:tail, :], preferred_element_type=F32)
        o_ref[...] = x_ref[...] + FFN_RESIDUAL_SCALE * (acc_ref[...] + last)


def _down(a, w_d, x, tm=1024, tn=1024, tk=2816):
    m, f = a.shape
    n = w_d.shape[1]
    nk = pl.cdiv(f, tk)
    tail = f - (nk - 1) * tk
    return pl.pallas_call(
        functools.partial(_down_kernel, nk=nk, tail=tail),
        grid=(m // tm, n // tn, nk),
        in_specs=[pl.BlockSpec((tm, tk), lambda i, j, k: (i, k)),
                  pl.BlockSpec((tk, tn), lambda i, j, k: (k, j)),
                  pl.BlockSpec((tm, tn), lambda i, j, k: (i, j))],
        out_specs=pl.BlockSpec((tm, tn), lambda i, j, k: (i, j)),
        out_shape=jax.ShapeDtypeStruct((m, n), F32),
        scratch_shapes=[pltpu.VMEM((tm, tn), F32)],
        compiler_params=_params(3),
        name="ffn_down",
    )(a, w_d, x)


def _ffn(x, norm_g, w_gate, w_up, w_down):
    h = _rmsnorm(x, norm_g, BF16)
    a = _gateup(h, w_gate, w_up)
    return _down(a, w_down.astype(BF16), x)


def _inproj_kernel(h_ref, wt_ref, o_ref, wt_bf, *, heads_per_block):
    i = pl.program_id(1)

    @pl.when(i == 0)
    def _():
        wt_bf[...] = wt_ref[...].astype(BF16)

    r = _dot_nt(h_ref[...], wt_bf[...])
    for c in range(heads_per_block):
        o_ref[c] = r[:, c * HEAD_DIM:(c + 1) * HEAD_DIM].astype(o_ref.dtype)


def _inproj(h, w_in_t, tm=1024, tn=1024):
    m, d = h.shape
    hpb = tn // HEAD_DIM
    return pl.pallas_call(
        functools.partial(_inproj_kernel, heads_per_block=hpb),
        grid=(QKV_WIDTH // tn, m // tm),
        in_specs=[pl.BlockSpec((tm, d), lambda j, i: (i, 0)),
                  pl.BlockSpec((tn, d), lambda j, i: (j, 0), pipeline_mode=pl.Buffered(1))],
        out_specs=pl.BlockSpec((hpb, tm, HEAD_DIM), lambda j, i: (j, i, 0)),
        out_shape=jax.ShapeDtypeStruct((QKV_WIDTH // HEAD_DIM, m, HEAD_DIM), BF16),
        scratch_shapes=[pltpu.VMEM((tn, d), BF16)],
        compiler_params=_params(2),
        name="in_proj",
    )(h, w_in_t)


def _split3(x):
    hi = x.astype(BF16)
    r = x - hi.astype(F32)
    mid = r.astype(BF16)
    lo = (r - mid.astype(F32)).astype(BF16)
    return hi, mid, lo


def _forget_kernel(h_ref, wt_ref, b_ref, o_ref, wt_bf, carry_ref, *, ts, n_valid):
    s = pl.program_id(1)
    valid = lax.broadcasted_iota(jnp.int32, (LANES, 1), 0) < n_valid

    @pl.when(jnp.logical_and(pl.program_id(0) == 0, s == 0))
    def _():
        wt_bf[...] = jnp.where(valid, wt_ref[...], 0.0).astype(BF16)

    @pl.when(s == 0)
    def _():
        carry_ref[...] = jnp.zeros_like(carry_ref)

    logit = _dot_nt(wt_bf[...], h_ref[...]) + b_ref[...]
    log_f = jnp.minimum(logit, 0.0) - jnp.log1p(jnp.exp(-jnp.abs(logit)))
    r = lax.broadcasted_iota(jnp.int32, (ts, ts), 0)
    c = lax.broadcasted_iota(jnp.int32, (ts, ts), 1)
    tri = (r <= c).astype(BF16)
    hi, mid, lo = _split3(log_f)
    cum = (jnp.dot(hi, tri, preferred_element_type=F32)
           + jnp.dot(mid, tri, preferred_element_type=F32)
           + jnp.dot(lo, tri, preferred_element_type=F32)) + carry_ref[...]
    o_ref[...] = cum
    carry_ref[...] = cum[:, ts - 1:ts]


def _forget_cumsum(h, w_in_t, b_col, batch, seq, ts=512):
    d = h.shape[1]
    ns = seq // ts
    n_valid = w_in_t.shape[0] - QKV_WIDTH
    return pl.pallas_call(
        functools.partial(_forget_kernel, ts=ts, n_valid=n_valid),
        grid=(batch, ns),
        in_specs=[pl.BlockSpec((ts, d), lambda b, s: (b * ns + s, 0)),
                  pl.BlockSpec((LANES, d), lambda b, s: (QKV_WIDTH // LANES, 0)),
                  pl.BlockSpec((LANES, 1), lambda b, s: (0, 0))],
        out_specs=pl.BlockSpec((None, LANES, ts), lambda b, s: (b, 0, s)),
        out_shape=jax.ShapeDtypeStruct((batch, LANES, seq), F32),
        scratch_shapes=[pltpu.VMEM((LANES, d), BF16), pltpu.VMEM((LANES, 1), F32)],
        compiler_params=_params(2),
        name="forget_cumsum",
    )(h, w_in_t, b_col)


def _head_rmsnorm(o, gain):
    return o * lax.rsqrt(jnp.mean(o * o, axis=-1, keepdims=True) + EPS) * gain


def _dot_nt(a, b):
    return lax.dot_general(a, b, (((1,), (1,)), ((), ())), preferred_element_type=F32)


def _fox_kernel(q_ref, k_ref, v_ref, qall_ref, c_ref, g_ref, o_ref, vaug_ref, first_ref, s_ref, m_ref, acc_ref,
                *, t, tk):
    i = pl.program_id(2)
    seq = c_ref.shape[1]

    def last_block(qi):
        return ((qi + 1) * t + tk - 1) // tk - 1

    @pl.when(i == 0)
    def _():
        vaug_ref[:, :HEAD_DIM] = v_ref[...]
        vaug_ref[:, HEAD_DIM:] = jnp.ones(v_ref.shape, BF16)
        def max_sq_norm(x_ref):
            xf = x_ref[...].astype(F32)
            return jnp.max(jnp.sum(xf * xf, axis=-1, keepdims=True), axis=0, keepdims=True)

        spread = 2.0 * NORM_BOUND_MARGIN * ATTN_SCALE * jnp.sqrt(max_sq_norm(qall_ref) * max_sq_norm(k_ref))
        c_all = c_ref[...]
        for qi in range(seq // t):
            dead = ((c_ref[:, qi * t:qi * t + 1] - c_all) + spread) * LOG2E < FOX_SKIP_BELOW
            n_dead = jnp.sum(dead.astype(F32)).astype(jnp.int32)
            first_ref[qi] = jnp.minimum(n_dead // tk, last_block(qi))

    q = q_ref[...]
    c_q0 = c_ref[:, pl.ds(pl.multiple_of(i * t, t), LANES)][:, 0:1]
    first = first_ref[i]
    last = last_block(i)

    def scores(j):
        start = pl.multiple_of(j * tk, tk)
        bias = (c_q0 - c_ref[:, pl.ds(start, tk)]) * LOG2E
        return _dot_nt(q, k_ref[pl.ds(start, tk), :]) * (ATTN_SCALE * LOG2E) + bias

    def step(j, cur, width):
        if width is None:
            s_ref[1 - cur] = scores(j + 1)
            s = s_ref[cur]
            width = tk
        else:
            row = i * t + lax.broadcasted_iota(jnp.int32, (t, width), 0)
            col = j * tk + lax.broadcasted_iota(jnp.int32, (t, width), 1)
            s = jnp.where(col <= row, s_ref[cur, :, :width], NEG_BIG)
        m_old = m_ref[...]
        m_new = jnp.maximum(m_old, jnp.max(s, axis=-1, keepdims=True))
        p = jnp.exp2(s - m_new).astype(BF16)
        vs = vaug_ref[pl.ds(pl.multiple_of(j * tk, tk), width), :]
        acc_ref[...] = jnp.exp2(m_old - m_new) * acc_ref[...] + jnp.dot(p, vs, preferred_element_type=F32)
        m_ref[...] = m_new

    def either_half(j, width):
        pl.when((j - first) % 2 == 0)(lambda: step(j, 0, width))
        pl.when((j - first) % 2 == 1)(lambda: step(j, 1, width))

    m_ref[...] = jnp.full(m_ref.shape, NEG_BIG, F32)
    acc_ref[...] = jnp.zeros(acc_ref.shape, F32)
    s_ref[0] = scores(first)

    @pl.loop(first, last)
    def _(j):
        either_half(j, None)

    reach = (i * t) % tk + t
    for width in range(t, tk + 1, t):
        pl.when(reach == width)(lambda width=width: either_half(last, width))
    acc = acc_ref[...]
    o_ref[...] = _head_rmsnorm(acc[:, :HEAD_DIM] / acc[:, HEAD_DIM:], g_ref[...]).astype(o_ref.dtype)


def _sb_kernel(q_ref, k_ref, v_ref, g_ref, o_ref, rem_ref, acc_ref, *, t, tk, heads):
    i = pl.program_id(2)
    tri = (lax.broadcasted_iota(jnp.int32, (tk, tk), 0)
           > lax.broadcasted_iota(jnp.int32, (tk, tk), 1)).astype(BF16)
    tri2 = jnp.concatenate([tri, tri], axis=0)

    def scores(h, j, r0, r1, keep_fn):
        z = _dot_nt(q_ref[h, r0:r1, :], k_ref[h, pl.ds(pl.multiple_of(j * tk, tk), tk), :]) * (ATTN_SCALE * LOG2E)
        ls_pos = jnp.minimum(z, 0.0) - jnp.log2(1.0 + jnp.exp2(-jnp.abs(z)))
        log_keep = ls_pos - z
        keep = None if keep_fn is None else keep_fn(j, r0, r1)
        if keep is not None:
            log_keep = jnp.where(keep, log_keep, 0.0)
        return ls_pos, log_keep, keep, jnp.sum(log_keep, axis=-1, keepdims=True)

    def suffix_sums(log_keep):
        hi = log_keep.astype(BF16)
        lo = (log_keep - hi.astype(F32)).astype(BF16)
        return jnp.dot(jnp.concatenate([hi, lo], axis=1), tri2, preferred_element_type=F32)

    def weighted_values(h, j, ls_pos, after, keep, rem):
        a = jnp.exp2(ls_pos + after + rem)
        if keep is not None:
            a = jnp.where(keep, a, 0.0)
        return jnp.dot(a.astype(BF16), v_ref[h, pl.ds(pl.multiple_of(j * tk, tk), tk), :],
                       preferred_element_type=F32)

    def block(h, j, r0, rem):
        ls_pos, log_keep, keep, tot = scores(h, j, r0, t, None)
        return weighted_values(h, j, ls_pos, suffix_sums(log_keep), keep, rem), tot

    def causal(j, r0, r1):
        return (j * tk + lax.broadcasted_iota(jnp.int32, (r1 - r0, tk), 1)
                < i * t + r0 + lax.broadcasted_iota(jnp.int32, (r1 - r0, tk), 0))

    def pad_rows(x, r0, r1):
        parts = [x]
        if r0:
            parts.insert(0, jnp.zeros((r0, x.shape[1]), F32))
        if r1 < t:
            parts.append(jnp.zeros((t - r1, x.shape[1]), F32))
        return jnp.concatenate(parts, axis=0) if len(parts) > 1 else x

    n_diag = t // tk
    first = (i + 1) * n_diag - 1
    extra = first - n_diag
    group = []
    for h in range(heads):
        for d in range(n_diag):
            group.append((h, first - d, t - (d + 1) * tk, t, causal))
        group.append((h, jnp.maximum(extra, 0), 0, tk, lambda j, r0, r1: extra >= 0))
    scored = [scores(*blk) for blk in group]
    afters = [suffix_sums(log_keep) for _, log_keep, _, _ in scored]
    rem = [jnp.zeros((t, 1), F32) for _ in range(heads)]
    rest_alive = [False] * heads
    outs = []
    for (h, j, r0, r1, _), (ls_pos, _, keep, tot), after in zip(group, scored, afters):
        if r1 < t:
            rest_alive[h] = jnp.max(rem[h][tk:]) > F32_EXP2_UNDERFLOW
        outs.append(weighted_values(h, j, ls_pos, after, keep, rem[h][r0:r1]))
        rem[h] = rem[h] + pad_rows(tot, r0, r1)
    for h in range(heads):
        acc_ref[h] = sum(pad_rows(out, r0, r1) for (hh, _, r0, r1, _), out in zip(group, outs) if hh == h)
        rem_ref[h] = rem[h]

    def cond(state):
        j, rem_max = state
        return jnp.logical_and(j >= 0, rem_max > F32_EXP2_UNDERFLOW)

    for h in range(heads):
        if tk < t:
            @pl.when(jnp.logical_and(rest_alive[h], extra >= 0))
            def _(h=h):
                out, tot = block(h, extra, tk, rem_ref[h, tk:])
                acc_ref[h, tk:] += out
                rem_ref[h, tk:] += tot

        def body(state, h=h):
            j, _ = state
            rem = rem_ref[h]
            out, tot = block(h, j, 0, rem)
            acc_ref[h] += out
            rem_ref[h] = rem + tot
            return j - 1, jnp.max(rem + tot)

        lax.while_loop(cond, body, (extra - 1, jnp.max(rem_ref[h])))
        o_ref[:, h * HEAD_DIM:(h + 1) * HEAD_DIM] = _head_rmsnorm(acc_ref[h], g_ref[h]).astype(o_ref.dtype)


def _attention_call(kernel_fn, name, qkvh, first_section, extra_inputs, extra_specs, gain, batch, seq, t,
                    scratch_shapes=(), heads=None):
    nq = seq // t
    per_step = heads or 1
    groups = N_HEADS // per_step

    def section_spec(section, rows):
        blocks_per_batch = seq // rows
        return pl.BlockSpec(
            (heads, rows, HEAD_DIM),
            lambda b, g, i: (section * groups + g, b * blocks_per_batch + (i if rows == t else 0), 0))

    return pl.pallas_call(
        kernel_fn,
        grid=(batch, groups, nq),
        in_specs=[section_spec(first_section, t), section_spec(first_section + 1, seq),
                  section_spec(first_section + 2, seq), *extra_specs,
                  pl.BlockSpec((heads, 1, HEAD_DIM), lambda b, g, i: (g, 0, 0))],
        out_specs=pl.BlockSpec((t, per_step * HEAD_DIM), lambda b, g, i: (b * nq + i, g)),
        out_shape=jax.ShapeDtypeStruct((batch * seq, GROUP_WIDTH), BF16),
        scratch_shapes=list(scratch_shapes),
        compiler_params=_params(3),
        name=name,
    )(qkvh, qkvh, qkvh, *extra_inputs, gain.reshape(N_HEADS, 1, HEAD_DIM))


def _attention(qkvh, c_rows, sb_gain, fox_gain, batch, seq, t_sb=512, tk_sb=256, heads_sb=4,
               t_fox=512, tk_fox=1024):
    assert t_sb % tk_sb == 0 and seq % t_sb == 0 and N_HEADS % heads_sb == 0
    sb_scratch = [pltpu.VMEM((heads_sb, t_sb, 1), F32), pltpu.VMEM((heads_sb, t_sb, HEAD_DIM), F32)]
    o_sb = _attention_call(functools.partial(_sb_kernel, t=t_sb, tk=tk_sb, heads=heads_sb), "sb_attention",
                           qkvh, 0, (), (), sb_gain, batch, seq, t_sb, sb_scratch, heads=heads_sb)
    assert seq % tk_fox == 0 and tk_fox % t_fox == 0
    fox_q_section = 3
    q_all_spec = pl.BlockSpec((None, seq, HEAD_DIM), lambda b, h, i: (fox_q_section * N_HEADS + h, b, 0))
    c_spec = pl.BlockSpec((None, None, 1, seq), lambda b, h, i: (b, h, 0, 0))
    fox_scratch = [pltpu.VMEM((seq, 2 * HEAD_DIM), BF16),
                   pltpu.SMEM((seq // t_fox,), jnp.int32),
                   pltpu.VMEM((2, t_fox, tk_fox), F32),
                   pltpu.VMEM((t_fox, 1), F32),
                   pltpu.VMEM((t_fox, 2 * HEAD_DIM), F32)]
    o_fox = _attention_call(functools.partial(_fox_kernel, t=t_fox, tk=tk_fox), "fox_attention", qkvh,
                            fox_q_section, (qkvh, c_rows), (q_all_spec, c_spec), fox_gain, batch, seq,
                            t_fox, fox_scratch)
    return o_sb, o_fox


def _outproj_kernel(a_ref, b_ref, w_ref, x_ref, o_ref):
    acc = jnp.dot(a_ref[...], w_ref[:GROUP_WIDTH, :], preferred_element_type=F32)
    acc = acc + jnp.dot(b_ref[...], w_ref[GROUP_WIDTH:, :], preferred_element_type=F32)
    o_ref[...] = x_ref[...] + acc


def _outproj(a, b, w_o, x, tm=1024, tn=1024):
    m = a.shape[0]
    n = w_o.shape[1]
    return pl.pallas_call(
        _outproj_kernel,
        grid=(m // tm, n // tn),
        in_specs=[pl.BlockSpec((tm, GROUP_WIDTH), lambda i, j: (i, 0)),
                  pl.BlockSpec((tm, GROUP_WIDTH), lambda i, j: (i, 0)),
                  pl.BlockSpec((2 * GROUP_WIDTH, tn), lambda i, j: (0, j)),
                  pl.BlockSpec((tm, tn), lambda i, j: (i, j))],
        out_specs=pl.BlockSpec((tm, tn), lambda i, j: (i, j)),
        out_shape=jax.ShapeDtypeStruct((m, n), F32),
        compiler_params=_params(2),
        name="out_proj",
    )(a, b, w_o, x)


def _mixer(x, norm_g, w_in, b_f, sb_g, fox_g, w_o, batch, seq):
    h = _rmsnorm(x, norm_g, BF16)
    w_in_t = w_in.T
    qkvh = _inproj(h, w_in_t)
    b_col = jnp.pad(b_f, (0, LANES - N_HEADS)).reshape(LANES, 1)
    c = _forget_cumsum(h, w_in_t, b_col, batch, seq)
    c_rows = c[:, :N_HEADS, :].reshape(batch, N_HEADS, 1, seq)
    o_sb, o_fox = _attention(qkvh, c_rows, sb_g, fox_g, batch, seq)
    return _outproj(o_sb, o_fox, w_o.astype(BF16), x)


@jax.jit
def kernel(x, norm_ffn1_g, ffn1_w_gate, ffn1_w_up, ffn1_w_down, norm_mix_g, w_in, b_f,
           sb_out_g, fox_out_g, w_o, norm_ffn2_g, ffn2_w_gate, ffn2_w_up, ffn2_w_down,
           norm_final_g):
    batch, seq, d = x.shape
    depth = w_in.shape[0]
    xf = x.reshape(batch * seq, d)
    for l in range(depth):
        xf = _ffn(xf, norm_ffn1_g[l], ffn1_w_gate[l], ffn1_w_up[l], ffn1_w_down[l])
        xf = _mixer(xf, norm_mix_g[l], w_in[l], b_f[l], sb_out_g[l], fox_out_g[l], w_o[l], batch, seq)
        xf = _ffn(xf, norm_ffn2_g[l], ffn2_w_gate[l], ffn2_w_up[l], ffn2_w_down[l])
    out = _rmsnorm(xf, norm_final_g, F32)
    return out.reshape(batch, seq, d)
```

```python
import functools
import math
from typing import NamedTuple, Optional

import jax
import jax.numpy as jnp
from jax import lax
from jax.experimental import pallas as pl
from jax.experimental.pallas import tpu as pltpu

D_MODEL = 4096
HEAD_DIM = 128
N_HEADS = 16
GROUP_WIDTH = N_HEADS * HEAD_DIM
QKV_WIDTH = 6 * GROUP_WIDTH
EPS = 1e-6
FFN_RESIDUAL_SCALE = 0.5
ATTN_SCALE = HEAD_DIM ** -0.5
LOG2E = math.log2(math.e)
NEG_BIG = -1e30
F32_EXP2_UNDERFLOW = -152.0
FOX_SKIP_BELOW = -160.0
NORM_BOUND_MARGIN = 1.01

LANES = 128
VMEM_LIMIT_BYTES = 58 * 1024 * 1024

F32 = jnp.float32
BF16 = jnp.bfloat16


def _params(n_axes):
    return pltpu.CompilerParams(
        dimension_semantics=("arbitrary",) * n_axes,
        vmem_limit_bytes=VMEM_LIMIT_BYTES,
    )


def _rmsnorm_kernel(x_ref, g_ref, o_ref):
    x = x_ref[...]
    y = x * lax.rsqrt(jnp.mean(x * x, axis=-1, keepdims=True) + EPS)
    o_ref[...] = (y * g_ref[...]).astype(o_ref.dtype)


def _rmsnorm(x, g, out_dtype, tm=256):
    m, d = x.shape
    return pl.pallas_call(
        _rmsnorm_kernel,
        grid=(m // tm,),
        in_specs=[pl.BlockSpec((tm, d), lambda i: (i, 0)),
                  pl.BlockSpec((1, d), lambda i: (0, 0))],
        out_specs=pl.BlockSpec((tm, d), lambda i: (i, 0)),
        out_shape=jax.ShapeDtypeStruct((m, d), out_dtype),
        compiler_params=_params(1),
        name="rmsnorm",
    )(x, g.reshape(1, d))


class Normed(NamedTuple):
    values: jax.Array
    sumsq: Optional[jax.Array]


def _emit_normed(y, gain_ref, values_ref, sumsq_ref, col_tile):
    values_ref[...] = (y * gain_ref[...]).astype(BF16)
    part = jnp.broadcast_to(jnp.sum(y * y, axis=-1, keepdims=True), sumsq_ref.shape)

    @pl.when(col_tile == 0)
    def _():
        sumsq_ref[...] = part

    @pl.when(col_tile > 0)
    def _():
        sumsq_ref[...] += part


def _rstd(sumsq, d):
    return lax.rsqrt(sumsq[:, :1] * (1.0 / d) + EPS)


def _normed_specs(tm, d, row_index):
    return [pl.BlockSpec((tm, d), lambda *g: (row_index(*g), 0)),
            pl.BlockSpec((tm, LANES), lambda *g: (row_index(*g), 0))]


def _gateup_kernel(*refs, tf, tail, deferred):
    if deferred:
        h_ref, ss_ref, wg_ref, wu_ref, o_ref, wg_bf, wu_bf = refs
    else:
        h_ref, wg_ref, wu_ref, o_ref, wg_bf, wu_bf = refs
    j = pl.program_id(0)
    i = pl.program_id(1)
    nj = pl.num_programs(0)

    def step(width):
        @pl.when(i == 0)
        def _():
            wg_bf[:, :width] = wg_ref[:, :width].astype(BF16)
            wu_bf[:, :width] = wu_ref[:, :width].astype(BF16)

        h = h_ref[...]
        gate = jnp.dot(h, wg_bf[:, :width], preferred_element_type=F32)
        up = jnp.dot(h, wu_bf[:, :width], preferred_element_type=F32)
        if deferred:
            rstd = _rstd(ss_ref[...], h_ref.shape[1])
            gate = gate * rstd
            up = up * rstd
        o_ref[:, :width] = (gate * (1.0 / (1.0 + jnp.exp(-gate))) * up).astype(o_ref.dtype)

    if tail == tf:
        step(tf)
    else:
        pl.when(j < nj - 1)(lambda: step(tf))
        pl.when(j == nj - 1)(lambda: step(tail))


def _gateup(h, w_gate, w_up, tm=1024, tf=512):
    m, d = h.values.shape
    f = w_gate.shape[1]
    nj = pl.cdiv(f, tf)
    tail = f - (nj - 1) * tf
    deferred = h.sumsq is not None
    h_specs = _normed_specs(tm, d, lambda j, i: i)[:2 if deferred else 1]
    w_spec = pl.BlockSpec((d, tf), lambda j, i: (0, j), pipeline_mode=pl.Buffered(1))
    return pl.pallas_call(
        functools.partial(_gateup_kernel, tf=tf, tail=tail, deferred=deferred),
        grid=(nj, m // tm),
        in_specs=[*h_specs, w_spec, w_spec],
        out_specs=pl.BlockSpec((tm, tf), lambda j, i: (i, j)),
        out_shape=jax.ShapeDtypeStruct((m, f), BF16),
        scratch_shapes=[pltpu.VMEM((d, tf), BF16), pltpu.VMEM((d, tf), BF16)],
        compiler_params=_params(2),
        name="ffn_gateup",
    )(*(h if deferred else h[:1]), w_gate, w_up)


def _down_kernel(*refs, nk, tail, next_norm):
    if next_norm:
        a_ref, w_ref, x_ref, gain_ref, o_ref, values_ref, sumsq_ref, acc_ref = refs
    else:
        a_ref, w_ref, x_ref, o_ref, acc_ref = refs
    k = pl.program_id(2)

    @pl.when(k == 0)
    def _():
        acc_ref[...] = jnp.zeros_like(acc_ref)

    @pl.when(k < nk - 1)
    def _():
        acc_ref[...] += jnp.dot(a_ref[...], w_ref[...], preferred_element_type=F32)

    @pl.when(k == nk - 1)
    def _():
        last = jnp.dot(a_ref[:, :tail], w_ref[:tail, :], preferred_element_type=F32)
        y = x_ref[...] + FFN_RESIDUAL_SCALE * (acc_ref[...] + last)
        o_ref[...] = y
        if next_norm:
            _emit_normed(y, gain_ref, values_ref, sumsq_ref, pl.program_id(1))


def _down(a, w_d, x, next_gain=None, tm=1024, tn=1024, tk=2816):
    m, f = a.shape
    n = w_d.shape[1]
    nk = pl.cdiv(f, tk)
    tail = f - (nk - 1) * tk
    next_norm = next_gain is not None
    tile = pl.BlockSpec((tm, tn), lambda i, j, k: (i, j))
    in_specs = [pl.BlockSpec((tm, tk), lambda i, j, k: (i, k)),
                pl.BlockSpec((tk, tn), lambda i, j, k: (k, j)), tile]
    out_specs, out_shape, operands = tile, jax.ShapeDtypeStruct((m, n), F32), [a, w_d, x]
    if next_norm:
        in_specs.append(pl.BlockSpec((1, tn), lambda i, j, k: (0, j)))
        operands.append(next_gain.reshape(1, n))
        out_specs = [tile, tile, pl.BlockSpec((tm, LANES), lambda i, j, k: (i, 0))]
        out_shape = [out_shape, jax.ShapeDtypeStruct((m, n), BF16), jax.ShapeDtypeStruct((m, LANES), F32)]
    out = pl.pallas_call(
        functools.partial(_down_kernel, nk=nk, tail=tail, next_norm=next_norm),
        grid=(m // tm, n // tn, nk),
        in_specs=in_specs,
        out_specs=out_specs,
        out_shape=out_shape,
        scratch_shapes=[pltpu.VMEM((tm, tn), F32)],
        compiler_params=_params(3),
        name="ffn_down",
    )(*operands)
    return (out[0], Normed(out[1], out[2])) if next_norm else out


def _ffn(x, h, w_gate, w_up, w_down, next_gain=None):
    a = _gateup(h, w_gate, w_up)
    return _down(a, w_down.astype(BF16), x, next_gain)


def _inproj_kernel(h_ref, ss_ref, wt_ref, o_ref, wt_bf, *, heads_per_block):
    i = pl.program_id(1)

    @pl.when(i == 0)
    def _():
        wt_bf[...] = wt_ref[...].astype(BF16)

    r = _dot_nt(h_ref[...], wt_bf[...]) * _rstd(ss_ref[...], h_ref.shape[1])
    for c in range(heads_per_block):
        o_ref[c] = r[:, c * HEAD_DIM:(c + 1) * HEAD_DIM].astype(o_ref.dtype)


def _inproj(h, w_in_t, tm=1024, tn=1024):
    m, d = h.values.shape
    hpb = tn // HEAD_DIM
    return pl.pallas_call(
        functools.partial(_inproj_kernel, heads_per_block=hpb),
        grid=(QKV_WIDTH // tn, m // tm),
        in_specs=[*_normed_specs(tm, d, lambda j, i: i),
                  pl.BlockSpec((tn, d), lambda j, i: (j, 0), pipeline_mode=pl.Buffered(1))],
        out_specs=pl.BlockSpec((hpb, tm, HEAD_DIM), lambda j, i: (j, i, 0)),
        out_shape=jax.ShapeDtypeStruct((QKV_WIDTH // HEAD_DIM, m, HEAD_DIM), BF16),
        scratch_shapes=[pltpu.VMEM((tn, d), BF16)],
        compiler_params=_params(2),
        name="in_proj",
    )(*h, w_in_t)


def _split3(x):
    hi = x.astype(BF16)
    r = x - hi.astype(F32)
    mid = r.astype(BF16)
    lo = (r - mid.astype(F32)).astype(BF16)
    return hi, mid, lo


def _forget_kernel(h_ref, ss_ref, wt_ref, b_ref, o_ref, wt_bf, carry_ref, *, ts, n_valid):
    s = pl.program_id(1)
    valid = lax.broadcasted_iota(jnp.int32, (LANES, 1), 0) < n_valid

    @pl.when(jnp.logical_and(pl.program_id(0) == 0, s == 0))
    def _():
        wt_bf[...] = jnp.where(valid, wt_ref[...], 0.0).astype(BF16)

    @pl.when(s == 0)
    def _():
        carry_ref[...] = jnp.zeros_like(carry_ref)

    rstd_row = lax.rsqrt(ss_ref[...].T[:1, :] * (1.0 / h_ref.shape[1]) + EPS)
    logit = _dot_nt(wt_bf[...], h_ref[...]) * rstd_row + b_ref[...]
    log_f = jnp.minimum(logit, 0.0) - jnp.log1p(jnp.exp(-jnp.abs(logit)))
    r = lax.broadcasted_iota(jnp.int32, (ts, ts), 0)
    c = lax.broadcasted_iota(jnp.int32, (ts, ts), 1)
    tri = (r <= c).astype(BF16)
    hi, mid, lo = _split3(log_f)
    cum = (jnp.dot(hi, tri, preferred_element_type=F32)
           + jnp.dot(mid, tri, preferred_element_type=F32)
           + jnp.dot(lo, tri, preferred_element_type=F32)) + carry_ref[...]
    o_ref[...] = cum
    carry_ref[...] = cum[:, ts - 1:ts]


def _forget_cumsum(h, w_in_t, b_col, batch, seq, ts=512):
    d = h.values.shape[1]
    ns = seq // ts
    n_valid = w_in_t.shape[0] - QKV_WIDTH
    return pl.pallas_call(
        functools.partial(_forget_kernel, ts=ts, n_valid=n_valid),
        grid=(batch, ns),
        in_specs=[*_normed_specs(ts, d, lambda b, s: b * ns + s),
                  pl.BlockSpec((LANES, d), lambda b, s: (QKV_WIDTH // LANES, 0)),
                  pl.BlockSpec((LANES, 1), lambda b, s: (0, 0))],
        out_specs=pl.BlockSpec((None, LANES, ts), lambda b, s: (b, 0, s)),
        out_shape=jax.ShapeDtypeStruct((batch, LANES, seq), F32),
        scratch_shapes=[pltpu.VMEM((LANES, d), BF16), pltpu.VMEM((LANES, 1), F32)],
        compiler_params=_params(2),
        name="forget_cumsum",
    )(*h, w_in_t, b_col)


def _head_rmsnorm(o, gain):
    return o * lax.rsqrt(jnp.mean(o * o, axis=-1, keepdims=True) + EPS) * gain


def _dot_nt(a, b):
    return lax.dot_general(a, b, (((1,), (1,)), ((), ())), preferred_element_type=F32)


def _fox_kernel(q_ref, k_ref, v_ref, qall_ref, c_ref, g_ref, o_ref, vaug_ref, first_ref, s_ref, m_ref, acc_ref,
                *, t, tk):
    i = pl.program_id(2)
    seq = c_ref.shape[1]

    def last_block(qi):
        return ((qi + 1) * t + tk - 1) // tk - 1

    @pl.when(i == 0)
    def _():
        vaug_ref[:, :HEAD_DIM] = v_ref[...]
        vaug_ref[:, HEAD_DIM:] = jnp.ones(v_ref.shape, BF16)
        def max_sq_norm(x_ref):
            xf = x_ref[...].astype(F32)
            return jnp.max(jnp.sum(xf * xf, axis=-1, keepdims=True), axis=0, keepdims=True)

        spread = 2.0 * NORM_BOUND_MARGIN * ATTN_SCALE * jnp.sqrt(max_sq_norm(qall_ref) * max_sq_norm(k_ref))
        c_all = c_ref[...]
        for qi in range(seq // t):
            dead = ((c_ref[:, qi * t:qi * t + 1] - c_all) + spread) * LOG2E < FOX_SKIP_BELOW
            n_dead = jnp.sum(dead.astype(F32)).astype(jnp.int32)
            first_ref[qi] = jnp.minimum(n_dead // tk, last_block(qi))

    q = q_ref[...]
    c_q0 = c_ref[:, pl.ds(pl.multiple_of(i * t, t), LANES)][:, 0:1]
    first = first_ref[i]
    last = last_block(i)

    def scores(j):
        start = pl.multiple_of(j * tk, tk)
        bias = (c_q0 - c_ref[:, pl.ds(start, tk)]) * LOG2E
        return _dot_nt(q, k_ref[pl.ds(start, tk), :]) * (ATTN_SCALE * LOG2E) + bias

    def step(j, cur, width):
        if width is None:
            s_ref[1 - cur] = scores(j + 1)
            s = s_ref[cur]
            width = tk
        else:
            row = i * t + lax.broadcasted_iota(jnp.int32, (t, width), 0)
            col = j * tk + lax.broadcasted_iota(jnp.int32, (t, width), 1)
            s = jnp.where(col <= row, s_ref[cur, :, :width], NEG_BIG)
        m_old = m_ref[...]
        m_new = jnp.maximum(m_old, jnp.max(s, axis=-1, keepdims=True))
        p = jnp.exp2(s - m_new).astype(BF16)
        vs = vaug_ref[pl.ds(pl.multiple_of(j * tk, tk), width), :]
        acc_ref[...] = jnp.exp2(m_old - m_new) * acc_ref[...] + jnp.dot(p, vs, preferred_element_type=F32)
        m_ref[...] = m_new

    def either_half(j, width):
        pl.when((j - first) % 2 == 0)(lambda: step(j, 0, width))
        pl.when((j - first) % 2 == 1)(lambda: step(j, 1, width))

    m_ref[...] = jnp.full(m_ref.shape, NEG_BIG, F32)
    acc_ref[...] = jnp.zeros(acc_ref.shape, F32)
    s_ref[0] = scores(first)

    @pl.loop(first, last)
    def _(j):
        either_half(j, None)

    reach = (i * t) % tk + t
    for width in range(t, tk + 1, t):
        pl.when(reach == width)(lambda width=width: either_half(last, width))
    acc = acc_ref[...]
    o_ref[...] = _head_rmsnorm(acc[:, :HEAD_DIM] / acc[:, HEAD_DIM:], g_ref[...]).astype(o_ref.dtype)


def _sb_kernel(q_ref, k_ref, v_ref, g_ref, o_ref, rem_ref, acc_ref, *, t, tk, heads):
    i = pl.program_id(2)
    tri = (lax.broadcasted_iota(jnp.int32, (tk, tk), 0)
           > lax.broadcasted_iota(jnp.int32, (tk, tk), 1)).astype(BF16)
    tri2 = jnp.concatenate([tri, tri], axis=0)

    def scores(h, j, r0, r1, keep_fn):
        z = _dot_nt(q_ref[h, r0:r1, :], k_ref[h, pl.ds(pl.multiple_of(j * tk, tk), tk), :]) * (ATTN_SCALE * LOG2E)
        ls_pos = jnp.minimum(z, 0.0) - jnp.log2(1.0 + jnp.exp2(-jnp.abs(z)))
        log_keep = ls_pos - z
        keep = None if keep_fn is None else keep_fn(j, r0, r1)
        if keep is not None:
            log_keep = jnp.where(keep, log_keep, 0.0)
        return ls_pos, log_keep, keep, jnp.sum(log_keep, axis=-1, keepdims=True)

    def suffix_sums(log_keep):
        hi = log_keep.astype(BF16)
        lo = (log_keep - hi.astype(F32)).astype(BF16)
        return jnp.dot(jnp.concatenate([hi, lo], axis=1), tri2, preferred_element_type=F32)

    def weighted_values(h, j, ls_pos, after, keep, rem):
        a = jnp.exp2(ls_pos + after + rem)
        if keep is not None:
            a = jnp.where(keep, a, 0.0)
        return jnp.dot(a.astype(BF16), v_ref[h, pl.ds(pl.multiple_of(j * tk, tk), tk), :],
                       preferred_element_type=F32)

    def block(h, j, r0, rem):
        ls_pos, log_keep, keep, tot = scores(h, j, r0, t, None)
        return weighted_values(h, j, ls_pos, suffix_sums(log_keep), keep, rem), tot

    def causal(j, r0, r1):
        return (j * tk + lax.broadcasted_iota(jnp.int32, (r1 - r0, tk), 1)
                < i * t + r0 + lax.broadcasted_iota(jnp.int32, (r1 - r0, tk), 0))

    def pad_rows(x, r0, r1):
        parts = [x]
        if r0:
            parts.insert(0, jnp.zeros((r0, x.shape[1]), F32))
        if r1 < t:
            parts.append(jnp.zeros((t - r1, x.shape[1]), F32))
        return jnp.concatenate(parts, axis=0) if len(parts) > 1 else x

    n_diag = t // tk
    first = (i + 1) * n_diag - 1
    extra = first - n_diag
    group = []
    for h in range(heads):
        for d in range(n_diag):
            group.append((h, first - d, t - (d + 1) * tk, t, causal))
        group.append((h, jnp.maximum(extra, 0), 0, tk, lambda j, r0, r1: extra >= 0))
    scored = [scores(*blk) for blk in group]
    afters = [suffix_sums(log_keep) for _, log_keep, _, _ in scored]
    rem = [jnp.zeros((t, 1), F32) for _ in range(heads)]
    rest_alive = [False] * heads
    outs = []
    for (h, j, r0, r1, _), (ls_pos, _, keep, tot), after in zip(group, scored, afters):
        if r1 < t:
            rest_alive[h] = jnp.max(rem[h][tk:]) > F32_EXP2_UNDERFLOW
        outs.append(weighted_values(h, j, ls_pos, after, keep, rem[h][r0:r1]))
        rem[h] = rem[h] + pad_rows(tot, r0, r1)
    for h in range(heads):
        acc_ref[h] = sum(pad_rows(out, r0, r1) for (hh, _, r0, r1, _), out in zip(group, outs) if hh == h)
        rem_ref[h] = rem[h]

    def cond(state):
        j, rem_max = state
        return jnp.logical_and(j >= 0, rem_max > F32_EXP2_UNDERFLOW)

    for h in range(heads):
        if tk < t:
            @pl.when(jnp.logical_and(rest_alive[h], extra >= 0))
            def _(h=h):
                out, tot = block(h, extra, tk, rem_ref[h, tk:])
                acc_ref[h, tk:] += out
                rem_ref[h, tk:] += tot

        def body(state, h=h):
            j, _ = state
            rem = rem_ref[h]
            out, tot = block(h, j, 0, rem)
            acc_ref[h] += out
            rem_ref[h] = rem + tot
            return j - 1, jnp.max(rem + tot)

        lax.while_loop(cond, body, (extra - 1, jnp.max(rem_ref[h])))
        o_ref[:, h * HEAD_DIM:(h + 1) * HEAD_DIM] = _head_rmsnorm(acc_ref[h], g_ref[h]).astype(o_ref.dtype)


def _attention_call(kernel_fn, name, qkvh, first_section, extra_inputs, extra_specs, gain, batch, seq, t,
                    scratch_shapes=(), heads=None):
    nq = seq // t
    per_step = heads or 1
    groups = N_HEADS // per_step

    def section_spec(section, rows):
        blocks_per_batch = seq // rows
        return pl.BlockSpec(
            (heads, rows, HEAD_DIM),
            lambda b, g, i: (section * groups + g, b * blocks_per_batch + (i if rows == t else 0), 0))

    return pl.pallas_call(
        kernel_fn,
        grid=(batch, groups, nq),
        in_specs=[section_spec(first_section, t), section_spec(first_section + 1, seq),
                  section_spec(first_section + 2, seq), *extra_specs,
                  pl.BlockSpec((heads, 1, HEAD_DIM), lambda b, g, i: (g, 0, 0))],
        out_specs=pl.BlockSpec((t, per_step * HEAD_DIM), lambda b, g, i: (b * nq + i, g)),
        out_shape=jax.ShapeDtypeStruct((batch * seq, GROUP_WIDTH), BF16),
        scratch_shapes=list(scratch_shapes),
        compiler_params=_params(3),
        name=name,
    )(qkvh, qkvh, qkvh, *extra_inputs, gain.reshape(N_HEADS, 1, HEAD_DIM))


def _attention(qkvh, c_rows, sb_gain, fox_gain, batch, seq, t_sb=512, tk_sb=256, heads_sb=4,
               t_fox=512, tk_fox=1024):
    assert t_sb % tk_sb == 0 and seq % t_sb == 0 and N_HEADS % heads_sb == 0
    sb_scratch = [pltpu.VMEM((heads_sb, t_sb, 1), F32), pltpu.VMEM((heads_sb, t_sb, HEAD_DIM), F32)]
    o_sb = _attention_call(functools.partial(_sb_kernel, t=t_sb, tk=tk_sb, heads=heads_sb), "sb_attention",
                           qkvh, 0, (), (), sb_gain, batch, seq, t_sb, sb_scratch, heads=heads_sb)
    assert seq % tk_fox == 0 and tk_fox % t_fox == 0
    fox_q_section = 3
    q_all_spec = pl.BlockSpec((None, seq, HEAD_DIM), lambda b, h, i: (fox_q_section * N_HEADS + h, b, 0))
    c_spec = pl.BlockSpec((None, None, 1, seq), lambda b, h, i: (b, h, 0, 0))
    fox_scratch = [pltpu.VMEM((seq, 2 * HEAD_DIM), BF16),
                   pltpu.SMEM((seq // t_fox,), jnp.int32),
                   pltpu.VMEM((2, t_fox, tk_fox), F32),
                   pltpu.VMEM((t_fox, 1), F32),
                   pltpu.VMEM((t_fox, 2 * HEAD_DIM), F32)]
    o_fox = _attention_call(functools.partial(_fox_kernel, t=t_fox, tk=tk_fox), "fox_attention", qkvh,
                            fox_q_section, (qkvh, c_rows), (q_all_spec, c_spec), fox_gain, batch, seq,
                            t_fox, fox_scratch)
    return o_sb, o_fox


def _outproj_kernel(a_ref, b_ref, w_ref, x_ref, gain_ref, o_ref, values_ref, sumsq_ref):
    acc = jnp.dot(a_ref[...], w_ref[:GROUP_WIDTH, :], preferred_element_type=F32)
    acc = acc + jnp.dot(b_ref[...], w_ref[GROUP_WIDTH:, :], preferred_element_type=F32)
    y = x_ref[...] + acc
    o_ref[...] = y
    _emit_normed(y, gain_ref, values_ref, sumsq_ref, pl.program_id(1))


def _outproj(a, b, w_o, x, next_gain, tm=1024, tn=512):
    m = a.shape[0]
    n = w_o.shape[1]
    tile = pl.BlockSpec((tm, tn), lambda i, j: (i, j))
    y, values, sumsq = pl.pallas_call(
        _outproj_kernel,
        grid=(m // tm, n // tn),
        in_specs=[pl.BlockSpec((tm, GROUP_WIDTH), lambda i, j: (i, 0)),
                  pl.BlockSpec((tm, GROUP_WIDTH), lambda i, j: (i, 0)),
                  pl.BlockSpec((2 * GROUP_WIDTH, tn), lambda i, j: (0, j)),
                  tile,
                  pl.BlockSpec((1, tn), lambda i, j: (0, j))],
        out_specs=[tile, tile, pl.BlockSpec((tm, LANES), lambda i, j: (i, 0))],
        out_shape=[jax.ShapeDtypeStruct((m, n), F32), jax.ShapeDtypeStruct((m, n), BF16),
                   jax.ShapeDtypeStruct((m, LANES), F32)],
        compiler_params=_params(2),
        name="out_proj",
    )(a, b, w_o, x, next_gain.reshape(1, n))
    return y, Normed(values, sumsq)


def _mixer(x, h, w_in, b_f, sb_g, fox_g, w_o, next_gain, batch, seq):
    w_in_t = w_in.T
    qkvh = _inproj(h, w_in_t)
    b_col = jnp.pad(b_f, (0, LANES - N_HEADS)).reshape(LANES, 1)
    c = _forget_cumsum(h, w_in_t, b_col, batch, seq)
    c_rows = c[:, :N_HEADS, :].reshape(batch, N_HEADS, 1, seq)
    o_sb, o_fox = _attention(qkvh, c_rows, sb_g, fox_g, batch, seq)
    return _outproj(o_sb, o_fox, w_o.astype(BF16), x, next_gain)


@jax.jit
def kernel(x, norm_ffn1_g, ffn1_w_gate, ffn1_w_up, ffn1_w_down, norm_mix_g, w_in, b_f,
           sb_out_g, fox_out_g, w_o, norm_ffn2_g, ffn2_w_gate, ffn2_w_up, ffn2_w_down,
           norm_final_g):
    batch, seq, d = x.shape
    depth = w_in.shape[0]
    xf = x.reshape(batch * seq, d)
    h = Normed(_rmsnorm(xf, norm_ffn1_g[0], BF16), None)
    for l in range(depth):
        xf, h = _ffn(xf, h, ffn1_w_gate[l], ffn1_w_up[l], ffn1_w_down[l], next_gain=norm_mix_g[l])
        xf, h = _mixer(xf, h, w_in[l], b_f[l], sb_out_g[l], fox_out_g[l], w_o[l], norm_ffn2_g[l], batch, seq)
        if l + 1 < depth:
            xf, h = _ffn(xf, h, ffn2_w_gate[l], ffn2_w_up[l], ffn2_w_down[l], next_gain=norm_ffn1_g[l + 1])
        else:
            xf = _ffn(xf, h, ffn2_w_gate[l], ffn2_w_up[l], ffn2_w_down[l])
    out = _rmsnorm(xf, norm_final_g, F32)
    return out.reshape(batch, seq, d)
```

```python
import functools
import math
from typing import NamedTuple, Optional

import jax
import jax.numpy as jnp
from jax import lax
from jax.experimental import pallas as pl
from jax.experimental.pallas import tpu as pltpu

D_MODEL = 4096
HEAD_DIM = 128
N_HEADS = 16
GROUP_WIDTH = N_HEADS * HEAD_DIM
QKV_WIDTH = 6 * GROUP_WIDTH
EPS = 1e-6
FFN_RESIDUAL_SCALE = 0.5
ATTN_SCALE = HEAD_DIM ** -0.5
LOG2E = math.log2(math.e)
NEG_BIG = -1e30
F32_EXP2_UNDERFLOW = -152.0
FOX_SKIP_BELOW = -160.0
NORM_BOUND_MARGIN = 1.01

LANES = 128
VMEM_LIMIT_BYTES = 58 * 1024 * 1024

F32 = jnp.float32
BF16 = jnp.bfloat16


def _params(n_axes):
    return pltpu.CompilerParams(
        dimension_semantics=("arbitrary",) * n_axes,
        vmem_limit_bytes=VMEM_LIMIT_BYTES,
    )


def _rmsnorm_kernel(x_ref, g_ref, o_ref):
    x = x_ref[...]
    y = x * lax.rsqrt(jnp.mean(x * x, axis=-1, keepdims=True) + EPS)
    o_ref[...] = (y * g_ref[...]).astype(o_ref.dtype)


def _rmsnorm(x, g, out_dtype, tm=256):
    m, d = x.shape
    return pl.pallas_call(
        _rmsnorm_kernel,
        grid=(m // tm,),
        in_specs=[pl.BlockSpec((tm, d), lambda i: (i, 0)),
                  pl.BlockSpec((1, d), lambda i: (0, 0))],
        out_specs=pl.BlockSpec((tm, d), lambda i: (i, 0)),
        out_shape=jax.ShapeDtypeStruct((m, d), out_dtype),
        compiler_params=_params(1),
        name="rmsnorm",
    )(x, g.reshape(1, d))


class Normed(NamedTuple):
    values: jax.Array
    sumsq: Optional[jax.Array]


def _emit_normed(y, gain_ref, values_ref, sumsq_ref, col_tile):
    values_ref[...] = (y * gain_ref[...]).astype(BF16)
    part = jnp.broadcast_to(jnp.sum(y * y, axis=-1, keepdims=True), sumsq_ref.shape)

    @pl.when(col_tile == 0)
    def _():
        sumsq_ref[...] = part

    @pl.when(col_tile > 0)
    def _():
        sumsq_ref[...] += part


def _rstd(sumsq, d):
    return lax.rsqrt(sumsq[:, :1] * (1.0 / d) + EPS)


def _normed_specs(tm, d, row_index):
    return [pl.BlockSpec((tm, d), lambda *g: (row_index(*g), 0)),
            pl.BlockSpec((tm, LANES), lambda *g: (row_index(*g), 0))]


def _gateup_kernel(*refs, tf, tail, deferred):
    if deferred:
        h_ref, ss_ref, wg_top, wg_bot, wu_top, wu_bot, o_ref = refs
    else:
        h_ref, wg_top, wg_bot, wu_top, wu_bot, o_ref = refs
    j = pl.program_id(0)
    nj = pl.num_programs(0)
    half = wg_top.shape[0]

    def step(width):
        h_top = h_ref[:, :half]
        h_bot = h_ref[:, half:]

        def project(top, bot):
            return (jnp.dot(h_top, top[:, :width].astype(BF16), preferred_element_type=F32)
                    + jnp.dot(h_bot, bot[:, :width].astype(BF16), preferred_element_type=F32))

        gate = project(wg_top, wg_bot)
        up = project(wu_top, wu_bot)
        if deferred:
            rstd = _rstd(ss_ref[...], h_ref.shape[1])
            gate = gate * rstd
            up = up * rstd
        o_ref[:, :width] = (gate * (1.0 / (1.0 + jnp.exp(-gate))) * up).astype(o_ref.dtype)

    if tail == tf:
        step(tf)
    else:
        pl.when(j < nj - 1)(lambda: step(tf))
        pl.when(j == nj - 1)(lambda: step(tail))


def _gateup(h, w_gate, w_up, tm=1024, tf=512):
    m, d = h.values.shape
    f = w_gate.shape[1]
    nj = pl.cdiv(f, tf)
    tail = f - (nj - 1) * tf
    deferred = h.sumsq is not None
    h_specs = _normed_specs(tm, d, lambda j, i: i)[:2 if deferred else 1]
    w_specs = [pl.BlockSpec((d // 2, tf), lambda j, i: (0, j)),
               pl.BlockSpec((d // 2, tf), lambda j, i: (1, j), pipeline_mode=pl.Buffered(1))]
    return pl.pallas_call(
        functools.partial(_gateup_kernel, tf=tf, tail=tail, deferred=deferred),
        grid=(nj, m // tm),
        in_specs=[*h_specs, *w_specs, *w_specs],
        out_specs=pl.BlockSpec((tm, tf), lambda j, i: (i, j)),
        out_shape=jax.ShapeDtypeStruct((m, f), BF16),
        compiler_params=_params(2),
        name="ffn_gateup",
    )(*(h if deferred else h[:1]), w_gate, w_gate, w_up, w_up)


def _down_kernel(*refs, nk, tail, next_norm):
    if next_norm:
        a_ref, w_ref, x_ref, gain_ref, o_ref, values_ref, sumsq_ref, acc_ref = refs
    else:
        a_ref, w_ref, x_ref, o_ref, acc_ref = refs
    k = pl.program_id(2)

    @pl.when(k == 0)
    def _():
        acc_ref[...] = jnp.zeros_like(acc_ref)

    @pl.when(k < nk - 1)
    def _():
        acc_ref[...] += jnp.dot(a_ref[...], w_ref[...], preferred_element_type=F32)

    @pl.when(k == nk - 1)
    def _():
        last = jnp.dot(a_ref[:, :tail], w_ref[:tail, :], preferred_element_type=F32)
        y = x_ref[...] + FFN_RESIDUAL_SCALE * (acc_ref[...] + last)
        o_ref[...] = y
        if next_norm:
            _emit_normed(y, gain_ref, values_ref, sumsq_ref, pl.program_id(1))


def _down(a, w_d, x, next_gain=None, tm=1024, tn=1024, tk=2816):
    m, f = a.shape
    n = w_d.shape[1]
    nk = pl.cdiv(f, tk)
    tail = f - (nk - 1) * tk
    next_norm = next_gain is not None
    tile = pl.BlockSpec((tm, tn), lambda i, j, k: (i, j))
    in_specs = [pl.BlockSpec((tm, tk), lambda i, j, k: (i, k)),
                pl.BlockSpec((tk, tn), lambda i, j, k: (k, j)), tile]
    out_specs, out_shape, operands = tile, jax.ShapeDtypeStruct((m, n), F32), [a, w_d, x]
    if next_norm:
        in_specs.append(pl.BlockSpec((1, tn), lambda i, j, k: (0, j)))
        operands.append(next_gain.reshape(1, n))
        out_specs = [tile, tile, pl.BlockSpec((tm, LANES), lambda i, j, k: (i, 0))]
        out_shape = [out_shape, jax.ShapeDtypeStruct((m, n), BF16), jax.ShapeDtypeStruct((m, LANES), F32)]
    out = pl.pallas_call(
        functools.partial(_down_kernel, nk=nk, tail=tail, next_norm=next_norm),
        grid=(m // tm, n // tn, nk),
        in_specs=in_specs,
        out_specs=out_specs,
        out_shape=out_shape,
        scratch_shapes=[pltpu.VMEM((tm, tn), F32)],
        compiler_params=_params(3),
        name="ffn_down",
    )(*operands)
    return (out[0], Normed(out[1], out[2])) if next_norm else out


def _ffn(x, h, w_gate, w_up, w_down, next_gain=None):
    a = _gateup(h, w_gate, w_up)
    return _down(a, w_down.astype(BF16), x, next_gain)


def _inproj_kernel(h_ref, ss_ref, wt_left, wt_right, o_ref, *, heads_per_block):
    half = wt_left.shape[1]
    r = (_dot_nt(h_ref[:, :half], wt_left[...].astype(BF16))
         + _dot_nt(h_ref[:, half:], wt_right[...].astype(BF16))) * _rstd(ss_ref[...], h_ref.shape[1])
    for c in range(heads_per_block):
        o_ref[c] = r[:, c * HEAD_DIM:(c + 1) * HEAD_DIM].astype(o_ref.dtype)


def _inproj(h, w_in_t, tm=1024, tn=1024):
    m, d = h.values.shape
    hpb = tn // HEAD_DIM
    return pl.pallas_call(
        functools.partial(_inproj_kernel, heads_per_block=hpb),
        grid=(QKV_WIDTH // tn, m // tm),
        in_specs=[*_normed_specs(tm, d, lambda j, i: i),
                  pl.BlockSpec((tn, d // 2), lambda j, i: (j, 0)),
                  pl.BlockSpec((tn, d // 2), lambda j, i: (j, 1), pipeline_mode=pl.Buffered(1))],
        out_specs=pl.BlockSpec((hpb, tm, HEAD_DIM), lambda j, i: (j, i, 0)),
        out_shape=jax.ShapeDtypeStruct((QKV_WIDTH // HEAD_DIM, m, HEAD_DIM), BF16),
        compiler_params=_params(2),
        name="in_proj",
    )(*h, w_in_t, w_in_t)


def _split3(x):
    hi = x.astype(BF16)
    r = x - hi.astype(F32)
    mid = r.astype(BF16)
    lo = (r - mid.astype(F32)).astype(BF16)
    return hi, mid, lo


def _forget_kernel(h_ref, ss_ref, wt_ref, b_ref, o_ref, wt_bf, carry_ref, *, ts, n_valid):
    s = pl.program_id(1)
    valid = lax.broadcasted_iota(jnp.int32, (LANES, 1), 0) < n_valid

    @pl.when(jnp.logical_and(pl.program_id(0) == 0, s == 0))
    def _():
        wt_bf[...] = jnp.where(valid, wt_ref[...], 0.0).astype(BF16)

    @pl.when(s == 0)
    def _():
        carry_ref[...] = jnp.zeros_like(carry_ref)

    rstd_row = lax.rsqrt(ss_ref[...].T[:1, :] * (1.0 / h_ref.shape[1]) + EPS)
    logit = _dot_nt(wt_bf[...], h_ref[...]) * rstd_row + b_ref[...]
    log_f = jnp.minimum(logit, 0.0) - jnp.log1p(jnp.exp(-jnp.abs(logit)))
    r = lax.broadcasted_iota(jnp.int32, (ts, ts), 0)
    c = lax.broadcasted_iota(jnp.int32, (ts, ts), 1)
    tri = (r <= c).astype(BF16)
    hi, mid, lo = _split3(log_f)
    cum = (jnp.dot(hi, tri, preferred_element_type=F32)
           + jnp.dot(mid, tri, preferred_element_type=F32)
           + jnp.dot(lo, tri, preferred_element_type=F32)) + carry_ref[...]
    o_ref[...] = cum
    carry_ref[...] = cum[:, ts - 1:ts]


def _forget_cumsum(h, w_in_t, b_col, batch, seq, ts=512):
    d = h.values.shape[1]
    ns = seq // ts
    n_valid = w_in_t.shape[0] - QKV_WIDTH
    return pl.pallas_call(
        functools.partial(_forget_kernel, ts=ts, n_valid=n_valid),
        grid=(batch, ns),
        in_specs=[*_normed_specs(ts, d, lambda b, s: b * ns + s),
                  pl.BlockSpec((LANES, d), lambda b, s: (QKV_WIDTH // LANES, 0)),
                  pl.BlockSpec((LANES, 1), lambda b, s: (0, 0))],
        out_specs=pl.BlockSpec((None, LANES, ts), lambda b, s: (b, 0, s)),
        out_shape=jax.ShapeDtypeStruct((batch, LANES, seq), F32),
        scratch_shapes=[pltpu.VMEM((LANES, d), BF16), pltpu.VMEM((LANES, 1), F32)],
        compiler_params=_params(2),
        name="forget_cumsum",
    )(*h, w_in_t, b_col)


def _head_rmsnorm(o, gain):
    return o * lax.rsqrt(jnp.mean(o * o, axis=-1, keepdims=True) + EPS) * gain


def _dot_nt(a, b):
    return lax.dot_general(a, b, (((1,), (1,)), ((), ())), preferred_element_type=F32)


def _fox_kernel(q_ref, k_ref, v_ref, qall_ref, c_ref, g_ref, o_ref, vaug_ref, first_ref, s_ref, m_ref, acc_ref,
                *, t, tk):
    i = pl.program_id(2)
    seq = c_ref.shape[1]

    def last_block(qi):
        return ((qi + 1) * t + tk - 1) // tk - 1

    @pl.when(i == 0)
    def _():
        vaug_ref[:, :HEAD_DIM] = v_ref[...]
        vaug_ref[:, HEAD_DIM:] = jnp.ones(v_ref.shape, BF16)
        def max_sq_norm(x_ref):
            xf = x_ref[...].astype(F32)
            return jnp.max(jnp.sum(xf * xf, axis=-1, keepdims=True), axis=0, keepdims=True)

        spread = 2.0 * NORM_BOUND_MARGIN * ATTN_SCALE * jnp.sqrt(max_sq_norm(qall_ref) * max_sq_norm(k_ref))
        c_all = c_ref[...]
        for qi in range(seq // t):
            dead = ((c_ref[:, qi * t:qi * t + 1] - c_all) + spread) * LOG2E < FOX_SKIP_BELOW
            n_dead = jnp.sum(dead.astype(F32)).astype(jnp.int32)
            first_ref[qi] = jnp.minimum(n_dead // tk, last_block(qi))

    q = q_ref[...]
    c_q0 = c_ref[:, pl.ds(pl.multiple_of(i * t, t), LANES)][:, 0:1]
    first = first_ref[i]
    last = last_block(i)

    def scores(j):
        start = pl.multiple_of(j * tk, tk)
        bias = (c_q0 - c_ref[:, pl.ds(start, tk)]) * LOG2E
        return _dot_nt(q, k_ref[pl.ds(start, tk), :]) * (ATTN_SCALE * LOG2E) + bias

    def step(j, cur, width):
        if width is None:
            s_ref[1 - cur] = scores(j + 1)
            s = s_ref[cur]
            width = tk
        else:
            row = i * t + lax.broadcasted_iota(jnp.int32, (t, width), 0)
            col = j * tk + lax.broadcasted_iota(jnp.int32, (t, width), 1)
            s = jnp.where(col <= row, s_ref[cur, :, :width], NEG_BIG)
        m_old = m_ref[...]
        m_new = jnp.maximum(m_old, jnp.max(s, axis=-1, keepdims=True))
        p = jnp.exp2(s - m_new).astype(BF16)
        vs = vaug_ref[pl.ds(pl.multiple_of(j * tk, tk), width), :]
        acc_ref[...] = jnp.exp2(m_old - m_new) * acc_ref[...] + jnp.dot(p, vs, preferred_element_type=F32)
        m_ref[...] = m_new

    def either_half(j, width):
        pl.when((j - first) % 2 == 0)(lambda: step(j, 0, width))
        pl.when((j - first) % 2 == 1)(lambda: step(j, 1, width))

    m_ref[...] = jnp.full(m_ref.shape, NEG_BIG, F32)
    acc_ref[...] = jnp.zeros(acc_ref.shape, F32)
    s_ref[0] = scores(first)

    @pl.loop(first, last)
    def _(j):
        either_half(j, None)

    reach = (i * t) % tk + t
    for width in range(t, tk + 1, t):
        pl.when(reach == width)(lambda width=width: either_half(last, width))
    acc = acc_ref[...]
    o_ref[...] = _head_rmsnorm(acc[:, :HEAD_DIM] / acc[:, HEAD_DIM:], g_ref[...]).astype(o_ref.dtype)


def _sb_kernel(q_ref, k_ref, v_ref, g_ref, o_ref, rem_ref, acc_ref, *, t, tk, heads):
    i = pl.program_id(2)
    tri = (lax.broadcasted_iota(jnp.int32, (tk, tk), 0)
           > lax.broadcasted_iota(jnp.int32, (tk, tk), 1)).astype(BF16)
    tri2 = jnp.concatenate([tri, tri], axis=0)

    def scores(h, j, r0, r1, keep_fn):
        z = _dot_nt(q_ref[h, r0:r1, :], k_ref[h, pl.ds(pl.multiple_of(j * tk, tk), tk), :]) * (ATTN_SCALE * LOG2E)
        ls_pos = jnp.minimum(z, 0.0) - jnp.log2(1.0 + jnp.exp2(-jnp.abs(z)))
        log_keep = ls_pos - z
        keep = None if keep_fn is None else keep_fn(j, r0, r1)
        if keep is not None:
            log_keep = jnp.where(keep, log_keep, 0.0)
        return ls_pos, log_keep, keep, jnp.sum(log_keep, axis=-1, keepdims=True)

    def suffix_sums(log_keep):
        hi = log_keep.astype(BF16)
        lo = (log_keep - hi.astype(F32)).astype(BF16)
        return jnp.dot(jnp.concatenate([hi, lo], axis=1), tri2, preferred_element_type=F32)

    def weighted_values(h, j, ls_pos, after, keep, rem):
        a = jnp.exp2(ls_pos + after + rem)
        if keep is not None:
            a = jnp.where(keep, a, 0.0)
        return jnp.dot(a.astype(BF16), v_ref[h, pl.ds(pl.multiple_of(j * tk, tk), tk), :],
                       preferred_element_type=F32)

    def block(h, j, r0, rem):
        ls_pos, log_keep, keep, tot = scores(h, j, r0, t, None)
        return weighted_values(h, j, ls_pos, suffix_sums(log_keep), keep, rem), tot

    def causal(j, r0, r1):
        return (j * tk + lax.broadcasted_iota(jnp.int32, (r1 - r0, tk), 1)
                < i * t + r0 + lax.broadcasted_iota(jnp.int32, (r1 - r0, tk), 0))

    def pad_rows(x, r0, r1):
        parts = [x]
        if r0:
            parts.insert(0, jnp.zeros((r0, x.shape[1]), F32))
        if r1 < t:
            parts.append(jnp.zeros((t - r1, x.shape[1]), F32))
        return jnp.concatenate(parts, axis=0) if len(parts) > 1 else x

    n_diag = t // tk
    first = (i + 1) * n_diag - 1
    extra = first - n_diag
    group = []
    for h in range(heads):
        for d in range(n_diag):
            group.append((h, first - d, t - (d + 1) * tk, t, causal))
        group.append((h, jnp.maximum(extra, 0), 0, tk, lambda j, r0, r1: extra >= 0))
    scored = [scores(*blk) for blk in group]
    afters = [suffix_sums(log_keep) for _, log_keep, _, _ in scored]
    rem = [jnp.zeros((t, 1), F32) for _ in range(heads)]
    rest_alive = [False] * heads
    outs = []
    for (h, j, r0, r1, _), (ls_pos, _, keep, tot), after in zip(group, scored, afters):
        if r1 < t:
            rest_alive[h] = jnp.max(rem[h][tk:]) > F32_EXP2_UNDERFLOW
        outs.append(weighted_values(h, j, ls_pos, after, keep, rem[h][r0:r1]))
        rem[h] = rem[h] + pad_rows(tot, r0, r1)
    for h in range(heads):
        acc_ref[h] = sum(pad_rows(out, r0, r1) for (hh, _, r0, r1, _), out in zip(group, outs) if hh == h)
        rem_ref[h] = rem[h]

    def cond(state):
        j, rem_max = state
        return jnp.logical_and(j >= 0, rem_max > F32_EXP2_UNDERFLOW)

    for h in range(heads):
        if tk < t:
            @pl.when(jnp.logical_and(rest_alive[h], extra >= 0))
            def _(h=h):
                out, tot = block(h, extra, tk, rem_ref[h, tk:])
                acc_ref[h, tk:] += out
                rem_ref[h, tk:] += tot

        def body(state, h=h):
            j, _ = state
            rem = rem_ref[h]
            out, tot = block(h, j, 0, rem)
            acc_ref[h] += out
            rem_ref[h] = rem + tot
            return j - 1, jnp.max(rem + tot)

        lax.while_loop(cond, body, (extra - 1, jnp.max(rem_ref[h])))
        o_ref[:, h * HEAD_DIM:(h + 1) * HEAD_DIM] = _head_rmsnorm(acc_ref[h], g_ref[h]).astype(o_ref.dtype)


def _attention_call(kernel_fn, name, qkvh, first_section, extra_inputs, extra_specs, gain, batch, seq, t,
                    scratch_shapes=(), heads=None):
    nq = seq // t
    per_step = heads or 1
    groups = N_HEADS // per_step

    def section_spec(section, rows):
        blocks_per_batch = seq // rows
        return pl.BlockSpec(
            (heads, rows, HEAD_DIM),
            lambda b, g, i: (section * groups + g, b * blocks_per_batch + (i if rows == t else 0), 0))

    return pl.pallas_call(
        kernel_fn,
        grid=(batch, groups, nq),
        in_specs=[section_spec(first_section, t), section_spec(first_section + 1, seq),
                  section_spec(first_section + 2, seq), *extra_specs,
                  pl.BlockSpec((heads, 1, HEAD_DIM), lambda b, g, i: (g, 0, 0))],
        out_specs=pl.BlockSpec((t, per_step * HEAD_DIM), lambda b, g, i: (b * nq + i, g)),
        out_shape=jax.ShapeDtypeStruct((batch * seq, GROUP_WIDTH), BF16),
        scratch_shapes=list(scratch_shapes),
        compiler_params=_params(3),
        name=name,
    )(qkvh, qkvh, qkvh, *extra_inputs, gain.reshape(N_HEADS, 1, HEAD_DIM))


def _attention(qkvh, c_rows, sb_gain, fox_gain, batch, seq, t_sb=512, tk_sb=256, heads_sb=4,
               t_fox=512, tk_fox=1024):
    assert t_sb % tk_sb == 0 and seq % t_sb == 0 and N_HEADS % heads_sb == 0
    sb_scratch = [pltpu.VMEM((heads_sb, t_sb, 1), F32), pltpu.VMEM((heads_sb, t_sb, HEAD_DIM), F32)]
    o_sb = _attention_call(functools.partial(_sb_kernel, t=t_sb, tk=tk_sb, heads=heads_sb), "sb_attention",
                           qkvh, 0, (), (), sb_gain, batch, seq, t_sb, sb_scratch, heads=heads_sb)
    assert seq % tk_fox == 0 and tk_fox % t_fox == 0
    fox_q_section = 3
    q_all_spec = pl.BlockSpec((None, seq, HEAD_DIM), lambda b, h, i: (fox_q_section * N_HEADS + h, b, 0))
    c_spec = pl.BlockSpec((None, None, 1, seq), lambda b, h, i: (b, h, 0, 0))
    fox_scratch = [pltpu.VMEM((seq, 2 * HEAD_DIM), BF16),
                   pltpu.SMEM((seq // t_fox,), jnp.int32),
                   pltpu.VMEM((2, t_fox, tk_fox), F32),
                   pltpu.VMEM((t_fox, 1), F32),
                   pltpu.VMEM((t_fox, 2 * HEAD_DIM), F32)]
    o_fox = _attention_call(functools.partial(_fox_kernel, t=t_fox, tk=tk_fox), "fox_attention", qkvh,
                            fox_q_section, (qkvh, c_rows), (q_all_spec, c_spec), fox_gain, batch, seq,
                            t_fox, fox_scratch)
    return o_sb, o_fox


def _outproj_kernel(a_ref, b_ref, w_ref, x_ref, gain_ref, o_ref, values_ref, sumsq_ref):
    acc = jnp.dot(a_ref[...], w_ref[:GROUP_WIDTH, :], preferred_element_type=F32)
    acc = acc + jnp.dot(b_ref[...], w_ref[GROUP_WIDTH:, :], preferred_element_type=F32)
    y = x_ref[...] + acc
    o_ref[...] = y
    _emit_normed(y, gain_ref, values_ref, sumsq_ref, pl.program_id(1))


def _outproj(a, b, w_o, x, next_gain, tm=1024, tn=512):
    m = a.shape[0]
    n = w_o.shape[1]
    tile = pl.BlockSpec((tm, tn), lambda i, j: (i, j))
    y, values, sumsq = pl.pallas_call(
        _outproj_kernel,
        grid=(m // tm, n // tn),
        in_specs=[pl.BlockSpec((tm, GROUP_WIDTH), lambda i, j: (i, 0)),
                  pl.BlockSpec((tm, GROUP_WIDTH), lambda i, j: (i, 0)),
                  pl.BlockSpec((2 * GROUP_WIDTH, tn), lambda i, j: (0, j)),
                  tile,
                  pl.BlockSpec((1, tn), lambda i, j: (0, j))],
        out_specs=[tile, tile, pl.BlockSpec((tm, LANES), lambda i, j: (i, 0))],
        out_shape=[jax.ShapeDtypeStruct((m, n), F32), jax.ShapeDtypeStruct((m, n), BF16),
                   jax.ShapeDtypeStruct((m, LANES), F32)],
        compiler_params=_params(2),
        name="out_proj",
    )(a, b, w_o, x, next_gain.reshape(1, n))
    return y, Normed(values, sumsq)


def _mixer(x, h, w_in, b_f, sb_g, fox_g, w_o, next_gain, batch, seq):
    w_in_t = w_in.T
    qkvh = _inproj(h, w_in_t)
    b_col = jnp.pad(b_f, (0, LANES - N_HEADS)).reshape(LANES, 1)
    c = _forget_cumsum(h, w_in_t, b_col, batch, seq)
    c_rows = c[:, :N_HEADS, :].reshape(batch, N_HEADS, 1, seq)
    o_sb, o_fox = _attention(qkvh, c_rows, sb_g, fox_g, batch, seq)
    return _outproj(o_sb, o_fox, w_o.astype(BF16), x, next_gain)


@jax.jit
def kernel(x, norm_ffn1_g, ffn1_w_gate, ffn1_w_up, ffn1_w_down, norm_mix_g, w_in, b_f,
           sb_out_g, fox_out_g, w_o, norm_ffn2_g, ffn2_w_gate, ffn2_w_up, ffn2_w_down,
           norm_final_g):
    batch, seq, d = x.shape
    depth = w_in.shape[0]
    xf = x.reshape(batch * seq, d)
    h = Normed(_rmsnorm(xf, norm_ffn1_g[0], BF16), None)
    for l in range(depth):
        xf, h = _ffn(xf, h, ffn1_w_gate[l], ffn1_w_up[l], ffn1_w_down[l], next_gain=norm_mix_g[l])
        xf, h = _mixer(xf, h, w_in[l], b_f[l], sb_out_g[l], fox_out_g[l], w_o[l], norm_ffn2_g[l], batch, seq)
        if l + 1 < depth:
            xf, h = _ffn(xf, h, ffn2_w_gate[l], ffn2_w_up[l], ffn2_w_down[l], next_gain=norm_ffn1_g[l + 1])
        else:
            xf = _ffn(xf, h, ffn2_w_gate[l], ffn2_w_up[l], ffn2_w_down[l])
    out = _rmsnorm(xf, norm_final_g, F32)
    return out.reshape(batch, seq, d)
```

```python
import functools
import math
from typing import NamedTuple, Optional

import jax
import jax.numpy as jnp
from jax import lax
from jax.experimental import pallas as pl
from jax.experimental.pallas import tpu as pltpu

D_MODEL = 4096
HEAD_DIM = 128
N_HEADS = 16
GROUP_WIDTH = N_HEADS * HEAD_DIM
QKV_WIDTH = 6 * GROUP_WIDTH
EPS = 1e-6
FFN_RESIDUAL_SCALE = 0.5
ATTN_SCALE = HEAD_DIM ** -0.5
LOG2E = math.log2(math.e)
NEG_BIG = -1e30
F32_EXP2_UNDERFLOW = -152.0
FOX_SKIP_BELOW = -160.0
NORM_BOUND_MARGIN = 1.01

LANES = 128
VMEM_LIMIT_BYTES = 58 * 1024 * 1024

F32 = jnp.float32
BF16 = jnp.bfloat16


def _params(n_axes):
    return pltpu.CompilerParams(
        dimension_semantics=("arbitrary",) * n_axes,
        vmem_limit_bytes=VMEM_LIMIT_BYTES,
    )


def _rmsnorm_kernel(x_ref, g_ref, o_ref):
    x = x_ref[...]
    y = x * lax.rsqrt(jnp.mean(x * x, axis=-1, keepdims=True) + EPS)
    o_ref[...] = (y * g_ref[...]).astype(o_ref.dtype)


def _rmsnorm(x, g, out_dtype, tm=256):
    m, d = x.shape
    return pl.pallas_call(
        _rmsnorm_kernel,
        grid=(m // tm,),
        in_specs=[pl.BlockSpec((tm, d), lambda i: (i, 0)),
                  pl.BlockSpec((1, d), lambda i: (0, 0))],
        out_specs=pl.BlockSpec((tm, d), lambda i: (i, 0)),
        out_shape=jax.ShapeDtypeStruct((m, d), out_dtype),
        compiler_params=_params(1),
        name="rmsnorm",
    )(x, g.reshape(1, d))


class Normed(NamedTuple):
    values: jax.Array
    sumsq: Optional[jax.Array]


def _emit_normed(y, gain_ref, values_ref, sumsq_ref, col_tile):
    values_ref[...] = (y * gain_ref[...]).astype(BF16)
    part = jnp.broadcast_to(jnp.sum(y * y, axis=-1, keepdims=True), sumsq_ref.shape)

    @pl.when(col_tile == 0)
    def _():
        sumsq_ref[...] = part

    @pl.when(col_tile > 0)
    def _():
        sumsq_ref[...] += part


def _rstd(sumsq, d):
    return lax.rsqrt(sumsq[:, :1] * (1.0 / d) + EPS)


def _normed_specs(tm, d, row_index):
    return [pl.BlockSpec((tm, d), lambda *g: (row_index(*g), 0)),
            pl.BlockSpec((tm, LANES), lambda *g: (row_index(*g), 0))]


def _gateup_kernel(*refs, tf, tail, deferred):
    if deferred:
        h_ref, ss_ref, wg_top, wg_bot, wu_top, wu_bot, o_ref = refs
    else:
        h_ref, wg_top, wg_bot, wu_top, wu_bot, o_ref = refs
    j = pl.program_id(0)
    nj = pl.num_programs(0)
    half = wg_top.shape[0]

    def step(width):
        h_top = h_ref[:, :half]
        h_bot = h_ref[:, half:]

        def project(top, bot):
            return (jnp.dot(h_top, top[:, :width].astype(BF16), preferred_element_type=F32)
                    + jnp.dot(h_bot, bot[:, :width].astype(BF16), preferred_element_type=F32))

        gate = project(wg_top, wg_bot)
        up = project(wu_top, wu_bot)
        if deferred:
            rstd = _rstd(ss_ref[...], h_ref.shape[1])
            gate = gate * rstd
            up = up * rstd
        o_ref[:, :width] = (gate * (1.0 / (1.0 + jnp.exp(-gate))) * up).astype(o_ref.dtype)

    if tail == tf:
        step(tf)
    else:
        pl.when(j < nj - 1)(lambda: step(tf))
        pl.when(j == nj - 1)(lambda: step(tail))


def _gateup(h, w_gate, w_up, tm=1024, tf=512):
    m, d = h.values.shape
    f = w_gate.shape[1]
    nj = pl.cdiv(f, tf)
    tail = f - (nj - 1) * tf
    deferred = h.sumsq is not None
    h_specs = _normed_specs(tm, d, lambda j, i: i)[:2 if deferred else 1]
    w_specs = [pl.BlockSpec((d // 2, tf), lambda j, i: (0, j)),
               pl.BlockSpec((d // 2, tf), lambda j, i: (1, j), pipeline_mode=pl.Buffered(1))]
    return pl.pallas_call(
        functools.partial(_gateup_kernel, tf=tf, tail=tail, deferred=deferred),
        grid=(nj, m // tm),
        in_specs=[*h_specs, *w_specs, *w_specs],
        out_specs=pl.BlockSpec((tm, tf), lambda j, i: (i, j)),
        out_shape=jax.ShapeDtypeStruct((m, f), BF16),
        compiler_params=_params(2),
        name="ffn_gateup",
    )(*(h if deferred else h[:1]), w_gate, w_gate, w_up, w_up)


def _down_kernel(*refs, nk, tail, next_norm):
    if next_norm:
        a_ref, w_ref, x_ref, gain_ref, o_ref, values_ref, sumsq_ref, acc_ref = refs
    else:
        a_ref, w_ref, x_ref, o_ref, acc_ref = refs
    k = pl.program_id(2)

    @pl.when(k == 0)
    def _():
        acc_ref[...] = jnp.zeros_like(acc_ref)

    @pl.when(k < nk - 1)
    def _():
        acc_ref[...] += jnp.dot(a_ref[...], w_ref[...], preferred_element_type=F32)

    @pl.when(k == nk - 1)
    def _():
        last = jnp.dot(a_ref[:, :tail], w_ref[:tail, :], preferred_element_type=F32)
        y = x_ref[...] + FFN_RESIDUAL_SCALE * (acc_ref[...] + last)
        o_ref[...] = y
        if next_norm:
            _emit_normed(y, gain_ref, values_ref, sumsq_ref, pl.program_id(1))


def _down(a, w_d, x, next_gain=None, tm=1024, tn=1024, tk=2816):
    m, f = a.shape
    n = w_d.shape[1]
    nk = pl.cdiv(f, tk)
    tail = f - (nk - 1) * tk
    next_norm = next_gain is not None
    tile = pl.BlockSpec((tm, tn), lambda i, j, k: (i, j))
    in_specs = [pl.BlockSpec((tm, tk), lambda i, j, k: (i, k)),
                pl.BlockSpec((tk, tn), lambda i, j, k: (k, j)), tile]
    out_specs, out_shape, operands = tile, jax.ShapeDtypeStruct((m, n), F32), [a, w_d, x]
    if next_norm:
        in_specs.append(pl.BlockSpec((1, tn), lambda i, j, k: (0, j)))
        operands.append(next_gain.reshape(1, n))
        out_specs = [tile, tile, pl.BlockSpec((tm, LANES), lambda i, j, k: (i, 0))]
        out_shape = [out_shape, jax.ShapeDtypeStruct((m, n), BF16), jax.ShapeDtypeStruct((m, LANES), F32)]
    out = pl.pallas_call(
        functools.partial(_down_kernel, nk=nk, tail=tail, next_norm=next_norm),
        grid=(m // tm, n // tn, nk),
        in_specs=in_specs,
        out_specs=out_specs,
        out_shape=out_shape,
        scratch_shapes=[pltpu.VMEM((tm, tn), F32)],
        compiler_params=_params(3),
        name="ffn_down",
    )(*operands)
    return (out[0], Normed(out[1], out[2])) if next_norm else out


def _ffn(x, h, w_gate, w_up, w_down, next_gain=None):
    a = _gateup(h, w_gate, w_up)
    return _down(a, w_down.astype(BF16), x, next_gain)


def _inproj_kernel(h_ref, ss_ref, wt_left, wt_right, o_ref, *, heads_per_block):
    half = wt_left.shape[1]
    r = (_dot_nt(h_ref[:, :half], wt_left[...].astype(BF16))
         + _dot_nt(h_ref[:, half:], wt_right[...].astype(BF16))) * _rstd(ss_ref[...], h_ref.shape[1])
    for c in range(heads_per_block):
        o_ref[c] = r[:, c * HEAD_DIM:(c + 1) * HEAD_DIM].astype(o_ref.dtype)


def _inproj(h, w_in_t, tm=1024, tn=1024):
    m, d = h.values.shape
    hpb = tn // HEAD_DIM
    return pl.pallas_call(
        functools.partial(_inproj_kernel, heads_per_block=hpb),
        grid=(QKV_WIDTH // tn, m // tm),
        in_specs=[*_normed_specs(tm, d, lambda j, i: i),
                  pl.BlockSpec((tn, d // 2), lambda j, i: (j, 0)),
                  pl.BlockSpec((tn, d // 2), lambda j, i: (j, 1), pipeline_mode=pl.Buffered(1))],
        out_specs=pl.BlockSpec((hpb, tm, HEAD_DIM), lambda j, i: (j, i, 0)),
        out_shape=jax.ShapeDtypeStruct((QKV_WIDTH // HEAD_DIM, m, HEAD_DIM), BF16),
        compiler_params=_params(2),
        name="in_proj",
    )(*h, w_in_t, w_in_t)


def _split3(x):
    hi = x.astype(BF16)
    r = x - hi.astype(F32)
    mid = r.astype(BF16)
    lo = (r - mid.astype(F32)).astype(BF16)
    return hi, mid, lo


def _forget_kernel(h_ref, ss_ref, wt_ref, b_ref, o_ref, wt_bf, carry_ref, *, ts, n_valid):
    s = pl.program_id(1)
    valid = lax.broadcasted_iota(jnp.int32, (LANES, 1), 0) < n_valid

    @pl.when(jnp.logical_and(pl.program_id(0) == 0, s == 0))
    def _():
        wt_bf[...] = jnp.where(valid, wt_ref[...], 0.0).astype(BF16)

    @pl.when(s == 0)
    def _():
        carry_ref[...] = jnp.zeros_like(carry_ref)

    rstd_row = lax.rsqrt(ss_ref[...].T[:1, :] * (1.0 / h_ref.shape[1]) + EPS)
    logit = _dot_nt(wt_bf[...], h_ref[...]) * rstd_row + b_ref[...]
    log_f = jnp.minimum(logit, 0.0) - jnp.log1p(jnp.exp(-jnp.abs(logit)))
    r = lax.broadcasted_iota(jnp.int32, (ts, ts), 0)
    c = lax.broadcasted_iota(jnp.int32, (ts, ts), 1)
    tri = (r <= c).astype(BF16)
    hi, mid, lo = _split3(log_f)
    cum = (jnp.dot(hi, tri, preferred_element_type=F32)
           + jnp.dot(mid, tri, preferred_element_type=F32)
           + jnp.dot(lo, tri, preferred_element_type=F32)) + carry_ref[...]
    o_ref[...] = cum
    carry_ref[...] = cum[:, ts - 1:ts]


def _forget_cumsum(h, w_in_t, b_col, batch, seq, ts=512):
    d = h.values.shape[1]
    ns = seq // ts
    n_valid = w_in_t.shape[0] - QKV_WIDTH
    return pl.pallas_call(
        functools.partial(_forget_kernel, ts=ts, n_valid=n_valid),
        grid=(batch, ns),
        in_specs=[*_normed_specs(ts, d, lambda b, s: b * ns + s),
                  pl.BlockSpec((LANES, d), lambda b, s: (QKV_WIDTH // LANES, 0)),
                  pl.BlockSpec((LANES, 1), lambda b, s: (0, 0))],
        out_specs=pl.BlockSpec((None, LANES, ts), lambda b, s: (b, 0, s)),
        out_shape=jax.ShapeDtypeStruct((batch, LANES, seq), F32),
        scratch_shapes=[pltpu.VMEM((LANES, d), BF16), pltpu.VMEM((LANES, 1), F32)],
        compiler_params=_params(2),
        name="forget_cumsum",
    )(*h, w_in_t, b_col)


def _head_rmsnorm(o, gain):
    return o * lax.rsqrt(jnp.mean(o * o, axis=-1, keepdims=True) + EPS) * gain


def _dot_nt(a, b):
    return lax.dot_general(a, b, (((1,), (1,)), ((), ())), preferred_element_type=F32)


def _fox_kernel(q_ref, k_ref, v_ref, qall_ref, c_ref, g_ref, o_ref, vaug_ref, first_ref, s_ref, m_ref, acc_ref,
                *, t, tk):
    i = pl.program_id(2)
    seq = c_ref.shape[1]

    def last_block(qi):
        return ((qi + 1) * t + tk - 1) // tk - 1

    @pl.when(i == 0)
    def _():
        vaug_ref[:, :HEAD_DIM] = v_ref[...]
        vaug_ref[:, HEAD_DIM:] = jnp.ones(v_ref.shape, BF16)
        def max_sq_norm(x_ref):
            xf = x_ref[...].astype(F32)
            return jnp.max(jnp.sum(xf * xf, axis=-1, keepdims=True), axis=0, keepdims=True)

        spread = 2.0 * NORM_BOUND_MARGIN * ATTN_SCALE * jnp.sqrt(max_sq_norm(qall_ref) * max_sq_norm(k_ref))
        c_all = c_ref[...]
        for qi in range(seq // t):
            dead = ((c_ref[:, qi * t:qi * t + 1] - c_all) + spread) * LOG2E < FOX_SKIP_BELOW
            n_dead = jnp.sum(dead.astype(F32)).astype(jnp.int32)
            first_ref[qi] = jnp.minimum(n_dead // tk, last_block(qi))

    q = q_ref[...]
    c_q0 = c_ref[:, pl.ds(pl.multiple_of(i * t, t), LANES)][:, 0:1]
    first = first_ref[i]
    last = last_block(i)

    def scores(j):
        start = pl.multiple_of(j * tk, tk)
        bias = (c_q0 - c_ref[:, pl.ds(start, tk)]) * LOG2E
        return _dot_nt(q, k_ref[pl.ds(start, tk), :]) * (ATTN_SCALE * LOG2E) + bias

    def step(j, cur, width):
        if width is None:
            s_ref[1 - cur] = scores(j + 1)
            s = s_ref[cur]
            width = tk
        else:
            row = i * t + lax.broadcasted_iota(jnp.int32, (t, width), 0)
            col = j * tk + lax.broadcasted_iota(jnp.int32, (t, width), 1)
            s = jnp.where(col <= row, s_ref[cur, :, :width], NEG_BIG)
        m_old = m_ref[...]
        m_new = jnp.maximum(m_old, jnp.max(s, axis=-1, keepdims=True))
        p = jnp.exp2(s - m_new).astype(BF16)
        vs = vaug_ref[pl.ds(pl.multiple_of(j * tk, tk), width), :]
        acc_ref[...] = jnp.exp2(m_old - m_new) * acc_ref[...] + jnp.dot(p, vs, preferred_element_type=F32)
        m_ref[...] = m_new

    def either_half(j, width):
        pl.when((j - first) % 2 == 0)(lambda: step(j, 0, width))
        pl.when((j - first) % 2 == 1)(lambda: step(j, 1, width))

    m_ref[...] = jnp.full(m_ref.shape, NEG_BIG, F32)
    acc_ref[...] = jnp.zeros(acc_ref.shape, F32)
    s_ref[0] = scores(first)

    @pl.loop(first, last)
    def _(j):
        either_half(j, None)

    reach = (i * t) % tk + t
    for width in range(t, tk + 1, t):
        pl.when(reach == width)(lambda width=width: either_half(last, width))
    acc = acc_ref[...]
    o_ref[...] = _head_rmsnorm(acc[:, :HEAD_DIM] / acc[:, HEAD_DIM:], g_ref[...]).astype(o_ref.dtype)


def _sb_kernel(q_ref, k_ref, v_ref, g_ref, o_ref, rem_ref, acc_ref, *, t, tk, heads):
    i = pl.program_id(2)
    tri = (lax.broadcasted_iota(jnp.int32, (tk, tk), 0)
           > lax.broadcasted_iota(jnp.int32, (tk, tk), 1)).astype(BF16)
    tri2 = jnp.concatenate([tri, tri], axis=0)

    def scores(h, j, r0, r1, keep_fn):
        z = _dot_nt(q_ref[h, r0:r1, :], k_ref[h, pl.ds(pl.multiple_of(j * tk, tk), tk), :]) * (ATTN_SCALE * LOG2E)
        if keep_fn is not None:
            z = jnp.where(keep_fn(j, r0, r1), z, NEG_BIG)
        ls_pos = jnp.minimum(z, 0.0) - jnp.log2(1.0 + jnp.exp2(-jnp.abs(z)))
        log_keep = ls_pos - z
        return ls_pos, log_keep, jnp.sum(log_keep, axis=-1, keepdims=True)

    def suffix_sums(log_keep):
        hi = log_keep.astype(BF16)
        lo = (log_keep - hi.astype(F32)).astype(BF16)
        return jnp.dot(jnp.concatenate([hi, lo], axis=1), tri2, preferred_element_type=F32)

    def weighted_values(h, j, ls_pos, after, rem):
        a = jnp.exp2(ls_pos + after + rem)
        return jnp.dot(a.astype(BF16), v_ref[h, pl.ds(pl.multiple_of(j * tk, tk), tk), :],
                       preferred_element_type=F32)

    def block(h, j, r0, rem):
        ls_pos, log_keep, tot = scores(h, j, r0, t, None)
        return weighted_values(h, j, ls_pos, suffix_sums(log_keep), rem), tot

    def causal(j, r0, r1):
        return (j * tk + lax.broadcasted_iota(jnp.int32, (r1 - r0, tk), 1)
                < i * t + r0 + lax.broadcasted_iota(jnp.int32, (r1 - r0, tk), 0))

    def pad_rows(x, r0, r1):
        parts = [x]
        if r0:
            parts.insert(0, jnp.zeros((r0, x.shape[1]), F32))
        if r1 < t:
            parts.append(jnp.zeros((t - r1, x.shape[1]), F32))
        return jnp.concatenate(parts, axis=0) if len(parts) > 1 else x

    n_diag = t // tk
    first = (i + 1) * n_diag - 1
    extra = first - n_diag
    group = []
    for h in range(heads):
        for d in range(n_diag):
            group.append((h, first - d, t - (d + 1) * tk, t, causal))
        group.append((h, jnp.maximum(extra, 0), 0, tk, lambda j, r0, r1: extra >= 0))
    scored = [scores(*blk) for blk in group]
    afters = [suffix_sums(log_keep) for _, log_keep, _ in scored]
    rem = [jnp.zeros((t, 1), F32) for _ in range(heads)]
    rest_alive = [False] * heads
    outs = []
    for (h, j, r0, r1, _), (ls_pos, _, tot), after in zip(group, scored, afters):
        if r1 < t:
            rest_alive[h] = jnp.max(rem[h][tk:]) > F32_EXP2_UNDERFLOW
        outs.append(weighted_values(h, j, ls_pos, after, rem[h][r0:r1]))
        rem[h] = rem[h] + pad_rows(tot, r0, r1)
    for h in range(heads):
        acc_ref[h] = sum(pad_rows(out, r0, r1) for (hh, _, r0, r1, _), out in zip(group, outs) if hh == h)
        rem_ref[h] = rem[h]

    def cond(state):
        j, rem_max = state
        return jnp.logical_and(j >= 0, rem_max > F32_EXP2_UNDERFLOW)

    for h in range(heads):
        if tk < t:
            @pl.when(jnp.logical_and(rest_alive[h], extra >= 0))
            def _(h=h):
                out, tot = block(h, extra, tk, rem_ref[h, tk:])
                acc_ref[h, tk:] += out
                rem_ref[h, tk:] += tot

        def body(state, h=h):
            j, _ = state
            rem = rem_ref[h]
            out, tot = block(h, j, 0, rem)
            acc_ref[h] += out
            rem_ref[h] = rem + tot
            return j - 1, jnp.max(rem + tot)

        lax.while_loop(cond, body, (extra - 1, jnp.max(rem_ref[h])))
        o_ref[:, h * HEAD_DIM:(h + 1) * HEAD_DIM] = _head_rmsnorm(acc_ref[h], g_ref[h]).astype(o_ref.dtype)


def _attention_call(kernel_fn, name, qkvh, first_section, extra_inputs, extra_specs, gain, batch, seq, t,
                    scratch_shapes=(), heads=None):
    nq = seq // t
    per_step = heads or 1
    groups = N_HEADS // per_step

    def section_spec(section, rows):
        blocks_per_batch = seq // rows
        return pl.BlockSpec(
            (heads, rows, HEAD_DIM),
            lambda b, g, i: (section * groups + g, b * blocks_per_batch + (i if rows == t else 0), 0))

    return pl.pallas_call(
        kernel_fn,
        grid=(batch, groups, nq),
        in_specs=[section_spec(first_section, t), section_spec(first_section + 1, seq),
                  section_spec(first_section + 2, seq), *extra_specs,
                  pl.BlockSpec((heads, 1, HEAD_DIM), lambda b, g, i: (g, 0, 0))],
        out_specs=pl.BlockSpec((t, per_step * HEAD_DIM), lambda b, g, i: (b * nq + i, g)),
        out_shape=jax.ShapeDtypeStruct((batch * seq, GROUP_WIDTH), BF16),
        scratch_shapes=list(scratch_shapes),
        compiler_params=_params(3),
        name=name,
    )(qkvh, qkvh, qkvh, *extra_inputs, gain.reshape(N_HEADS, 1, HEAD_DIM))


def _attention(qkvh, c_rows, sb_gain, fox_gain, batch, seq, t_sb=512, tk_sb=256, heads_sb=4,
               t_fox=1024, tk_fox=1024):
    assert t_sb % tk_sb == 0 and seq % t_sb == 0 and N_HEADS % heads_sb == 0
    sb_scratch = [pltpu.VMEM((heads_sb, t_sb, 1), F32), pltpu.VMEM((heads_sb, t_sb, HEAD_DIM), F32)]
    o_sb = _attention_call(functools.partial(_sb_kernel, t=t_sb, tk=tk_sb, heads=heads_sb), "sb_attention",
                           qkvh, 0, (), (), sb_gain, batch, seq, t_sb, sb_scratch, heads=heads_sb)
    assert seq % tk_fox == 0 and tk_fox % t_fox == 0
    fox_q_section = 3
    q_all_spec = pl.BlockSpec((None, seq, HEAD_DIM), lambda b, h, i: (fox_q_section * N_HEADS + h, b, 0))
    c_spec = pl.BlockSpec((None, None, 1, seq), lambda b, h, i: (b, h, 0, 0))
    fox_scratch = [pltpu.VMEM((seq, 2 * HEAD_DIM), BF16),
                   pltpu.SMEM((seq // t_fox,), jnp.int32),
                   pltpu.VMEM((2, t_fox, tk_fox), F32),
                   pltpu.VMEM((t_fox, 1), F32),
                   pltpu.VMEM((t_fox, 2 * HEAD_DIM), F32)]
    o_fox = _attention_call(functools.partial(_fox_kernel, t=t_fox, tk=tk_fox), "fox_attention", qkvh,
                            fox_q_section, (qkvh, c_rows), (q_all_spec, c_spec), fox_gain, batch, seq,
                            t_fox, fox_scratch)
    return o_sb, o_fox


def _outproj_kernel(a_ref, b_ref, w_ref, x_ref, gain_ref, o_ref, values_ref, sumsq_ref):
    acc = jnp.dot(a_ref[...], w_ref[:GROUP_WIDTH, :], preferred_element_type=F32)
    acc = acc + jnp.dot(b_ref[...], w_ref[GROUP_WIDTH:, :], preferred_element_type=F32)
    y = x_ref[...] + acc
    o_ref[...] = y
    _emit_normed(y, gain_ref, values_ref, sumsq_ref, pl.program_id(1))


def _outproj(a, b, w_o, x, next_gain, tm=1024, tn=512):
    m = a.shape[0]
    n = w_o.shape[1]
    tile = pl.BlockSpec((tm, tn), lambda i, j: (i, j))
    y, values, sumsq = pl.pallas_call(
        _outproj_kernel,
        grid=(m // tm, n // tn),
        in_specs=[pl.BlockSpec((tm, GROUP_WIDTH), lambda i, j: (i, 0)),
                  pl.BlockSpec((tm, GROUP_WIDTH), lambda i, j: (i, 0)),
                  pl.BlockSpec((2 * GROUP_WIDTH, tn), lambda i, j: (0, j)),
                  tile,
                  pl.BlockSpec((1, tn), lambda i, j: (0, j))],
        out_specs=[tile, tile, pl.BlockSpec((tm, LANES), lambda i, j: (i, 0))],
        out_shape=[jax.ShapeDtypeStruct((m, n), F32), jax.ShapeDtypeStruct((m, n), BF16),
                   jax.ShapeDtypeStruct((m, LANES), F32)],
        compiler_params=_params(2),
        name="out_proj",
    )(a, b, w_o, x, next_gain.reshape(1, n))
    return y, Normed(values, sumsq)


def _mixer(x, h, w_in, b_f, sb_g, fox_g, w_o, next_gain, batch, seq):
    w_in_t = w_in.T
    qkvh = _inproj(h, w_in_t)
    b_col = jnp.pad(b_f, (0, LANES - N_HEADS)).reshape(LANES, 1)
    c = _forget_cumsum(h, w_in_t, b_col, batch, seq)
    c_rows = c[:, :N_HEADS, :].reshape(batch, N_HEADS, 1, seq)
    o_sb, o_fox = _attention(qkvh, c_rows, sb_g, fox_g, batch, seq)
    return _outproj(o_sb, o_fox, w_o.astype(BF16), x, next_gain)


@jax.jit
def kernel(x, norm_ffn1_g, ffn1_w_gate, ffn1_w_up, ffn1_w_down, norm_mix_g, w_in, b_f,
           sb_out_g, fox_out_g, w_o, norm_ffn2_g, ffn2_w_gate, ffn2_w_up, ffn2_w_down,
           norm_final_g):
    batch, seq, d = x.shape
    depth = w_in.shape[0]
    xf = x.reshape(batch * seq, d)
    h = Normed(_rmsnorm(xf, norm_ffn1_g[0], BF16), None)
    for l in range(depth):
        xf, h = _ffn(xf, h, ffn1_w_gate[l], ffn1_w_up[l], ffn1_w_down[l], next_gain=norm_mix_g[l])
        xf, h = _mixer(xf, h, w_in[l], b_f[l], sb_out_g[l], fox_out_g[l], w_o[l], norm_ffn2_g[l], batch, seq)
        if l + 1 < depth:
            xf, h = _ffn(xf, h, ffn2_w_gate[l], ffn2_w_up[l], ffn2_w_down[l], next_gain=norm_ffn1_g[l + 1])
        else:
            xf = _ffn(xf, h, ffn2_w_gate[l], ffn2_w_up[l], ffn2_w_down[l])
    out = _rmsnorm(xf, norm_final_g, F32)
    return out.reshape(batch, seq, d)
```

```python
import functools
import math
from typing import NamedTuple, Optional

import jax
import jax.numpy as jnp
from jax import lax
from jax.experimental import pallas as pl
from jax.experimental.pallas import tpu as pltpu

D_MODEL = 4096
HEAD_DIM = 128
N_HEADS = 16
GROUP_WIDTH = N_HEADS * HEAD_DIM
QKV_WIDTH = 6 * GROUP_WIDTH
EPS = 1e-6
FFN_RESIDUAL_SCALE = 0.5
ATTN_SCALE = HEAD_DIM ** -0.5
LOG2E = math.log2(math.e)
NEG_BIG = -1e30
F32_EXP2_UNDERFLOW = -152.0
FOX_SKIP_BELOW = -160.0
NORM_BOUND_MARGIN = 1.01
FOX_DIAGONAL_BANDS = 4

LANES = 128
VMEM_LIMIT_BYTES = 58 * 1024 * 1024

F32 = jnp.float32
BF16 = jnp.bfloat16


def _params(n_axes):
    return pltpu.CompilerParams(
        dimension_semantics=("arbitrary",) * n_axes,
        vmem_limit_bytes=VMEM_LIMIT_BYTES,
    )


def _rmsnorm_kernel(x_ref, g_ref, o_ref):
    x = x_ref[...]
    y = x * lax.rsqrt(jnp.mean(x * x, axis=-1, keepdims=True) + EPS)
    o_ref[...] = (y * g_ref[...]).astype(o_ref.dtype)


def _rmsnorm(x, g, out_dtype, tm=256):
    m, d = x.shape
    return pl.pallas_call(
        _rmsnorm_kernel,
        grid=(m // tm,),
        in_specs=[pl.BlockSpec((tm, d), lambda i: (i, 0)),
                  pl.BlockSpec((1, d), lambda i: (0, 0))],
        out_specs=pl.BlockSpec((tm, d), lambda i: (i, 0)),
        out_shape=jax.ShapeDtypeStruct((m, d), out_dtype),
        compiler_params=_params(1),
        name="rmsnorm",
    )(x, g.reshape(1, d))


class Normed(NamedTuple):
    values: jax.Array
    sumsq: Optional[jax.Array]


def _emit_normed(y, gain_ref, values_ref, sumsq_ref, col_tile):
    values_ref[...] = (y * gain_ref[...]).astype(BF16)
    part = jnp.broadcast_to(jnp.sum(y * y, axis=-1, keepdims=True), sumsq_ref.shape)

    @pl.when(col_tile == 0)
    def _():
        sumsq_ref[...] = part

    @pl.when(col_tile > 0)
    def _():
        sumsq_ref[...] += part


def _rstd(sumsq, d):
    return lax.rsqrt(sumsq[:, :1] * (1.0 / d) + EPS)


def _normed_specs(tm, d, row_index):
    return [pl.BlockSpec((tm, d), lambda *g: (row_index(*g), 0)),
            pl.BlockSpec((tm, LANES), lambda *g: (row_index(*g), 0))]


def _gateup_kernel(*refs, tf, tail, deferred):
    if deferred:
        h_ref, ss_ref, wg_top, wg_bot, wu_top, wu_bot, o_ref = refs
    else:
        h_ref, wg_top, wg_bot, wu_top, wu_bot, o_ref = refs
    j = pl.program_id(0)
    nj = pl.num_programs(0)
    half = wg_top.shape[0]

    def step(width):
        h_top = h_ref[:, :half]
        h_bot = h_ref[:, half:]

        def project(top, bot):
            return (jnp.dot(h_top, top[:, :width].astype(BF16), preferred_element_type=F32)
                    + jnp.dot(h_bot, bot[:, :width].astype(BF16), preferred_element_type=F32))

        gate = project(wg_top, wg_bot)
        up = project(wu_top, wu_bot)
        if deferred:
            rstd = _rstd(ss_ref[...], h_ref.shape[1])
            gate = gate * rstd
            up = up * rstd
        o_ref[:, :width] = (gate * (1.0 / (1.0 + jnp.exp(-gate))) * up).astype(o_ref.dtype)

    if tail == tf:
        step(tf)
    else:
        pl.when(j < nj - 1)(lambda: step(tf))
        pl.when(j == nj - 1)(lambda: step(tail))


def _gateup(h, w_gate, w_up, tm=1024, tf=512):
    m, d = h.values.shape
    f = w_gate.shape[1]
    nj = pl.cdiv(f, tf)
    tail = f - (nj - 1) * tf
    deferred = h.sumsq is not None
    h_specs = _normed_specs(tm, d, lambda j, i: i)[:2 if deferred else 1]
    w_specs = [pl.BlockSpec((d // 2, tf), lambda j, i: (0, j)),
               pl.BlockSpec((d // 2, tf), lambda j, i: (1, j), pipeline_mode=pl.Buffered(1))]
    return pl.pallas_call(
        functools.partial(_gateup_kernel, tf=tf, tail=tail, deferred=deferred),
        grid=(nj, m // tm),
        in_specs=[*h_specs, *w_specs, *w_specs],
        out_specs=pl.BlockSpec((tm, tf), lambda j, i: (i, j)),
        out_shape=jax.ShapeDtypeStruct((m, f), BF16),
        compiler_params=_params(2),
        name="ffn_gateup",
    )(*(h if deferred else h[:1]), w_gate, w_gate, w_up, w_up)


def _down_kernel(*refs, nk, tail, next_norm):
    if next_norm:
        a_ref, w_ref, x_ref, gain_ref, o_ref, values_ref, sumsq_ref, acc_ref = refs
    else:
        a_ref, w_ref, x_ref, o_ref, acc_ref = refs
    k = pl.program_id(2)

    @pl.when(k == 0)
    def _():
        acc_ref[...] = jnp.zeros_like(acc_ref)

    @pl.when(k < nk - 1)
    def _():
        acc_ref[...] += jnp.dot(a_ref[...], w_ref[...], preferred_element_type=F32)

    @pl.when(k == nk - 1)
    def _():
        last = jnp.dot(a_ref[:, :tail], w_ref[:tail, :], preferred_element_type=F32)
        y = x_ref[...] + FFN_RESIDUAL_SCALE * (acc_ref[...] + last)
        o_ref[...] = y
        if next_norm:
            _emit_normed(y, gain_ref, values_ref, sumsq_ref, pl.program_id(1))


def _down(a, w_d, x, next_gain=None, tm=1024, tn=1024, tk=2816):
    m, f = a.shape
    n = w_d.shape[1]
    nk = pl.cdiv(f, tk)
    tail = f - (nk - 1) * tk
    next_norm = next_gain is not None
    tile = pl.BlockSpec((tm, tn), lambda i, j, k: (i, j))
    in_specs = [pl.BlockSpec((tm, tk), lambda i, j, k: (i, k)),
                pl.BlockSpec((tk, tn), lambda i, j, k: (k, j)), tile]
    out_specs, out_shape, operands = tile, jax.ShapeDtypeStruct((m, n), F32), [a, w_d, x]
    if next_norm:
        in_specs.append(pl.BlockSpec((1, tn), lambda i, j, k: (0, j)))
        operands.append(next_gain.reshape(1, n))
        out_specs = [tile, tile, pl.BlockSpec((tm, LANES), lambda i, j, k: (i, 0))]
        out_shape = [out_shape, jax.ShapeDtypeStruct((m, n), BF16), jax.ShapeDtypeStruct((m, LANES), F32)]
    out = pl.pallas_call(
        functools.partial(_down_kernel, nk=nk, tail=tail, next_norm=next_norm),
        grid=(m // tm, n // tn, nk),
        in_specs=in_specs,
        out_specs=out_specs,
        out_shape=out_shape,
        scratch_shapes=[pltpu.VMEM((tm, tn), F32)],
        compiler_params=_params(3),
        name="ffn_down",
    )(*operands)
    return (out[0], Normed(out[1], out[2])) if next_norm else out


def _ffn(x, h, w_gate, w_up, w_down, next_gain=None):
    a = _gateup(h, w_gate, w_up)
    return _down(a, w_down.astype(BF16), x, next_gain)


def _inproj_kernel(h_ref, ss_ref, wt_left, wt_right, o_ref, *, heads_per_block):
    half = wt_left.shape[1]
    r = (_dot_nt(h_ref[:, :half], wt_left[...].astype(BF16))
         + _dot_nt(h_ref[:, half:], wt_right[...].astype(BF16))) * _rstd(ss_ref[...], h_ref.shape[1])
    for c in range(heads_per_block):
        o_ref[c] = r[:, c * HEAD_DIM:(c + 1) * HEAD_DIM].astype(o_ref.dtype)


def _inproj(h, w_in_t, tm=1024, tn=1024):
    m, d = h.values.shape
    hpb = tn // HEAD_DIM
    return pl.pallas_call(
        functools.partial(_inproj_kernel, heads_per_block=hpb),
        grid=(QKV_WIDTH // tn, m // tm),
        in_specs=[*_normed_specs(tm, d, lambda j, i: i),
                  pl.BlockSpec((tn, d // 2), lambda j, i: (j, 0)),
                  pl.BlockSpec((tn, d // 2), lambda j, i: (j, 1), pipeline_mode=pl.Buffered(1))],
        out_specs=pl.BlockSpec((hpb, tm, HEAD_DIM), lambda j, i: (j, i, 0)),
        out_shape=jax.ShapeDtypeStruct((QKV_WIDTH // HEAD_DIM, m, HEAD_DIM), BF16),
        compiler_params=_params(2),
        name="in_proj",
    )(*h, w_in_t, w_in_t)


def _split3(x):
    hi = x.astype(BF16)
    r = x - hi.astype(F32)
    mid = r.astype(BF16)
    lo = (r - mid.astype(F32)).astype(BF16)
    return hi, mid, lo


def _forget_kernel(h_ref, ss_ref, wt_ref, b_ref, o_ref, wt_bf, carry_ref, *, ts, n_valid):
    s = pl.program_id(1)
    valid = lax.broadcasted_iota(jnp.int32, (LANES, 1), 0) < n_valid

    @pl.when(jnp.logical_and(pl.program_id(0) == 0, s == 0))
    def _():
        wt_bf[...] = jnp.where(valid, wt_ref[...], 0.0).astype(BF16)

    @pl.when(s == 0)
    def _():
        carry_ref[...] = jnp.zeros_like(carry_ref)

    rstd_row = lax.rsqrt(ss_ref[...].T[:1, :] * (1.0 / h_ref.shape[1]) + EPS)
    logit = _dot_nt(wt_bf[...], h_ref[...]) * rstd_row + b_ref[...]
    log_f = jnp.minimum(logit, 0.0) - jnp.log1p(jnp.exp(-jnp.abs(logit)))
    r = lax.broadcasted_iota(jnp.int32, (ts, ts), 0)
    c = lax.broadcasted_iota(jnp.int32, (ts, ts), 1)
    tri = (r <= c).astype(BF16)
    hi, mid, lo = _split3(log_f)
    cum = (jnp.dot(hi, tri, preferred_element_type=F32)
           + jnp.dot(mid, tri, preferred_element_type=F32)
           + jnp.dot(lo, tri, preferred_element_type=F32)) + carry_ref[...]
    o_ref[...] = cum
    carry_ref[...] = cum[:, ts - 1:ts]


def _forget_cumsum(h, w_in_t, b_col, batch, seq, ts=512):
    d = h.values.shape[1]
    ns = seq // ts
    n_valid = w_in_t.shape[0] - QKV_WIDTH
    return pl.pallas_call(
        functools.partial(_forget_kernel, ts=ts, n_valid=n_valid),
        grid=(batch, ns),
        in_specs=[*_normed_specs(ts, d, lambda b, s: b * ns + s),
                  pl.BlockSpec((LANES, d), lambda b, s: (QKV_WIDTH // LANES, 0)),
                  pl.BlockSpec((LANES, 1), lambda b, s: (0, 0))],
        out_specs=pl.BlockSpec((None, LANES, ts), lambda b, s: (b, 0, s)),
        out_shape=jax.ShapeDtypeStruct((batch, LANES, seq), F32),
        scratch_shapes=[pltpu.VMEM((LANES, d), BF16), pltpu.VMEM((LANES, 1), F32)],
        compiler_params=_params(2),
        name="forget_cumsum",
    )(*h, w_in_t, b_col)


def _head_rmsnorm(o, gain):
    return o * lax.rsqrt(jnp.mean(o * o, axis=-1, keepdims=True) + EPS) * gain


def _dot_nt(a, b):
    return lax.dot_general(a, b, (((1,), (1,)), ((), ())), preferred_element_type=F32)


def _fox_kernel(q_ref, k_ref, v_ref, qall_ref, c_ref, g_ref, o_ref, vaug_ref, first_ref, s_ref, m_ref, acc_ref,
                *, t, tk):
    i = pl.program_id(2)
    seq = c_ref.shape[1]

    def last_block(qi):
        return ((qi + 1) * t + tk - 1) // tk - 1

    @pl.when(i == 0)
    def _():
        vaug_ref[:, :HEAD_DIM] = v_ref[...]
        vaug_ref[:, HEAD_DIM:] = jnp.ones(v_ref.shape, BF16)
        def max_sq_norm(x_ref):
            xf = x_ref[...].astype(F32)
            return jnp.max(jnp.sum(xf * xf, axis=-1, keepdims=True), axis=0, keepdims=True)

        spread = 2.0 * NORM_BOUND_MARGIN * ATTN_SCALE * jnp.sqrt(max_sq_norm(qall_ref) * max_sq_norm(k_ref))
        c_all = c_ref[...]
        for qi in range(seq // t):
            dead = ((c_ref[:, qi * t:qi * t + 1] - c_all) + spread) * LOG2E < FOX_SKIP_BELOW
            n_dead = jnp.sum(dead.astype(F32)).astype(jnp.int32)
            first_ref[qi] = jnp.minimum(n_dead // tk, last_block(qi))

    q = q_ref[...]
    c_q0 = c_ref[:, pl.ds(pl.multiple_of(i * t, t), LANES)][:, 0:1]
    first = first_ref[i]
    last = last_block(i)

    def scores(j):
        start = pl.multiple_of(j * tk, tk)
        bias = (c_q0 - c_ref[:, pl.ds(start, tk)]) * LOG2E
        return _dot_nt(q, k_ref[pl.ds(start, tk), :]) * (ATTN_SCALE * LOG2E) + bias

    def step(j, cur, width):
        def update(r0, r1, cols, masked):
            s = s_ref[cur, r0:r1, :cols]
            if masked:
                row = i * t + r0 + lax.broadcasted_iota(jnp.int32, (r1 - r0, cols), 0)
                col = j * tk + lax.broadcasted_iota(jnp.int32, (r1 - r0, cols), 1)
                s = jnp.where(col <= row, s, NEG_BIG)
            m_old = m_ref[r0:r1]
            m_new = jnp.maximum(m_old, jnp.max(s, axis=-1, keepdims=True))
            p = jnp.exp2(s - m_new).astype(BF16)
            vs = vaug_ref[pl.ds(pl.multiple_of(j * tk, tk), cols), :]
            acc_ref[r0:r1] = (jnp.exp2(m_old - m_new) * acc_ref[r0:r1]
                              + jnp.dot(p, vs, preferred_element_type=F32))
            m_ref[r0:r1] = m_new

        if width is None:
            s_ref[1 - cur] = scores(j + 1)
            update(0, t, tk, False)
        else:
            band = t // FOX_DIAGONAL_BANDS
            for r0 in range(0, t, band):
                update(r0, r0 + band, width - t + r0 + band, True)

    def either_half(j, width):
        pl.when((j - first) % 2 == 0)(lambda: step(j, 0, width))
        pl.when((j - first) % 2 == 1)(lambda: step(j, 1, width))

    m_ref[...] = jnp.full(m_ref.shape, NEG_BIG, F32)
    acc_ref[...] = jnp.zeros(acc_ref.shape, F32)
    s_ref[0] = scores(first)

    @pl.loop(first, last)
    def _(j):
        either_half(j, None)

    reach = (i * t) % tk + t
    for width in range(t, tk + 1, t):
        pl.when(reach == width)(lambda width=width: either_half(last, width))
    acc = acc_ref[...]
    o_ref[...] = _head_rmsnorm(acc[:, :HEAD_DIM] / acc[:, HEAD_DIM:], g_ref[...]).astype(o_ref.dtype)


def _sb_kernel(q_ref, k_ref, v_ref, g_ref, o_ref, rem_ref, acc_ref, *, t, tk, heads):
    i = pl.program_id(2)
    tri = (lax.broadcasted_iota(jnp.int32, (tk, tk), 0)
           > lax.broadcasted_iota(jnp.int32, (tk, tk), 1)).astype(BF16)
    tri2 = jnp.concatenate([tri, tri], axis=0)

    def scores(h, j, r0, r1, keep_fn):
        z = _dot_nt(q_ref[h, r0:r1, :], k_ref[h, pl.ds(pl.multiple_of(j * tk, tk), tk), :]) * (ATTN_SCALE * LOG2E)
        if keep_fn is not None:
            z = jnp.where(keep_fn(j, r0, r1), z, NEG_BIG)
        ls_pos = jnp.minimum(z, 0.0) - jnp.log2(1.0 + jnp.exp2(-jnp.abs(z)))
        log_keep = ls_pos - z
        return ls_pos, log_keep, jnp.sum(log_keep, axis=-1, keepdims=True)

    def suffix_sums(log_keep):
        hi = log_keep.astype(BF16)
        lo = (log_keep - hi.astype(F32)).astype(BF16)
        return jnp.dot(jnp.concatenate([hi, lo], axis=1), tri2, preferred_element_type=F32)

    def weighted_values(h, j, ls_pos, after, rem):
        a = jnp.exp2(ls_pos + after + rem)
        return jnp.dot(a.astype(BF16), v_ref[h, pl.ds(pl.multiple_of(j * tk, tk), tk), :],
                       preferred_element_type=F32)

    def block(h, j, r0, rem):
        ls_pos, log_keep, tot = scores(h, j, r0, t, None)
        return weighted_values(h, j, ls_pos, suffix_sums(log_keep), rem), tot

    def causal(j, r0, r1):
        return (j * tk + lax.broadcasted_iota(jnp.int32, (r1 - r0, tk), 1)
                < i * t + r0 + lax.broadcasted_iota(jnp.int32, (r1 - r0, tk), 0))

    def pad_rows(x, r0, r1):
        parts = [x]
        if r0:
            parts.insert(0, jnp.zeros((r0, x.shape[1]), F32))
        if r1 < t:
            parts.append(jnp.zeros((t - r1, x.shape[1]), F32))
        return jnp.concatenate(parts, axis=0) if len(parts) > 1 else x

    n_diag = t // tk
    first = (i + 1) * n_diag - 1
    extra = first - n_diag
    group = []
    for h in range(heads):
        for d in range(n_diag):
            group.append((h, first - d, t - (d + 1) * tk, t, causal))
        group.append((h, jnp.maximum(extra, 0), 0, tk, lambda j, r0, r1: extra >= 0))
    scored = [scores(*blk) for blk in group]
    afters = [suffix_sums(log_keep) for _, log_keep, _ in scored]
    rem = [jnp.zeros((t, 1), F32) for _ in range(heads)]
    rest_alive = [False] * heads
    outs = []
    for (h, j, r0, r1, _), (ls_pos, _, tot), after in zip(group, scored, afters):
        if r1 < t:
            rest_alive[h] = jnp.max(rem[h][tk:]) > F32_EXP2_UNDERFLOW
        outs.append(weighted_values(h, j, ls_pos, after, rem[h][r0:r1]))
        rem[h] = rem[h] + pad_rows(tot, r0, r1)
    for h in range(heads):
        acc_ref[h] = sum(pad_rows(out, r0, r1) for (hh, _, r0, r1, _), out in zip(group, outs) if hh == h)
        rem_ref[h] = rem[h]

    def cond(state):
        j, rem_max = state
        return jnp.logical_and(j >= 0, rem_max > F32_EXP2_UNDERFLOW)

    for h in range(heads):
        if tk < t:
            @pl.when(jnp.logical_and(rest_alive[h], extra >= 0))
            def _(h=h):
                out, tot = block(h, extra, tk, rem_ref[h, tk:])
                acc_ref[h, tk:] += out
                rem_ref[h, tk:] += tot

        def body(state, h=h):
            j, _ = state
            rem = rem_ref[h]
            out, tot = block(h, j, 0, rem)
            acc_ref[h] += out
            rem_ref[h] = rem + tot
            return j - 1, jnp.max(rem + tot)

        lax.while_loop(cond, body, (extra - 1, jnp.max(rem_ref[h])))
        o_ref[:, h * HEAD_DIM:(h + 1) * HEAD_DIM] = _head_rmsnorm(acc_ref[h], g_ref[h]).astype(o_ref.dtype)


def _attention_call(kernel_fn, name, qkvh, first_section, extra_inputs, extra_specs, gain, batch, seq, t,
                    scratch_shapes=(), heads=None):
    nq = seq // t
    per_step = heads or 1
    groups = N_HEADS // per_step

    def section_spec(section, rows):
        blocks_per_batch = seq // rows
        return pl.BlockSpec(
            (heads, rows, HEAD_DIM),
            lambda b, g, i: (section * groups + g, b * blocks_per_batch + (i if rows == t else 0), 0))

    return pl.pallas_call(
        kernel_fn,
        grid=(batch, groups, nq),
        in_specs=[section_spec(first_section, t), section_spec(first_section + 1, seq),
                  section_spec(first_section + 2, seq), *extra_specs,
                  pl.BlockSpec((heads, 1, HEAD_DIM), lambda b, g, i: (g, 0, 0))],
        out_specs=pl.BlockSpec((t, per_step * HEAD_DIM), lambda b, g, i: (b * nq + i, g)),
        out_shape=jax.ShapeDtypeStruct((batch * seq, GROUP_WIDTH), BF16),
        scratch_shapes=list(scratch_shapes),
        compiler_params=_params(3),
        name=name,
    )(qkvh, qkvh, qkvh, *extra_inputs, gain.reshape(N_HEADS, 1, HEAD_DIM))


def _attention(qkvh, c_rows, sb_gain, fox_gain, batch, seq, t_sb=512, tk_sb=256, heads_sb=4,
               t_fox=1024, tk_fox=1024):
    assert t_sb % tk_sb == 0 and seq % t_sb == 0 and N_HEADS % heads_sb == 0
    sb_scratch = [pltpu.VMEM((heads_sb, t_sb, 1), F32), pltpu.VMEM((heads_sb, t_sb, HEAD_DIM), F32)]
    o_sb = _attention_call(functools.partial(_sb_kernel, t=t_sb, tk=tk_sb, heads=heads_sb), "sb_attention",
                           qkvh, 0, (), (), sb_gain, batch, seq, t_sb, sb_scratch, heads=heads_sb)
    assert seq % tk_fox == 0 and tk_fox % t_fox == 0
    fox_q_section = 3
    q_all_spec = pl.BlockSpec((None, seq, HEAD_DIM), lambda b, h, i: (fox_q_section * N_HEADS + h, b, 0))
    c_spec = pl.BlockSpec((None, None, 1, seq), lambda b, h, i: (b, h, 0, 0))
    fox_scratch = [pltpu.VMEM((seq, 2 * HEAD_DIM), BF16),
                   pltpu.SMEM((seq // t_fox,), jnp.int32),
                   pltpu.VMEM((2, t_fox, tk_fox), F32),
                   pltpu.VMEM((t_fox, 1), F32),
                   pltpu.VMEM((t_fox, 2 * HEAD_DIM), F32)]
    o_fox = _attention_call(functools.partial(_fox_kernel, t=t_fox, tk=tk_fox), "fox_attention", qkvh,
                            fox_q_section, (qkvh, c_rows), (q_all_spec, c_spec), fox_gain, batch, seq,
                            t_fox, fox_scratch)
    return o_sb, o_fox


def _outproj_kernel(a_ref, b_ref, w_ref, x_ref, gain_ref, o_ref, values_ref, sumsq_ref):
    acc = jnp.dot(a_ref[...], w_ref[:GROUP_WIDTH, :], preferred_element_type=F32)
    acc = acc + jnp.dot(b_ref[...], w_ref[GROUP_WIDTH:, :], preferred_element_type=F32)
    y = x_ref[...] + acc
    o_ref[...] = y
    _emit_normed(y, gain_ref, values_ref, sumsq_ref, pl.program_id(1))


def _outproj(a, b, w_o, x, next_gain, tm=1024, tn=512):
    m = a.shape[0]
    n = w_o.shape[1]
    tile = pl.BlockSpec((tm, tn), lambda i, j: (i, j))
    y, values, sumsq = pl.pallas_call(
        _outproj_kernel,
        grid=(m // tm, n // tn),
        in_specs=[pl.BlockSpec((tm, GROUP_WIDTH), lambda i, j: (i, 0)),
                  pl.BlockSpec((tm, GROUP_WIDTH), lambda i, j: (i, 0)),
                  pl.BlockSpec((2 * GROUP_WIDTH, tn), lambda i, j: (0, j)),
                  tile,
                  pl.BlockSpec((1, tn), lambda i, j: (0, j))],
        out_specs=[tile, tile, pl.BlockSpec((tm, LANES), lambda i, j: (i, 0))],
        out_shape=[jax.ShapeDtypeStruct((m, n), F32), jax.ShapeDtypeStruct((m, n), BF16),
                   jax.ShapeDtypeStruct((m, LANES), F32)],
        compiler_params=_params(2),
        name="out_proj",
    )(a, b, w_o, x, next_gain.reshape(1, n))
    return y, Normed(values, sumsq)


def _mixer(x, h, w_in, b_f, sb_g, fox_g, w_o, next_gain, batch, seq):
    w_in_t = w_in.T
    qkvh = _inproj(h, w_in_t)
    b_col = jnp.pad(b_f, (0, LANES - N_HEADS)).reshape(LANES, 1)
    c = _forget_cumsum(h, w_in_t, b_col, batch, seq)
    c_rows = c[:, :N_HEADS, :].reshape(batch, N_HEADS, 1, seq)
    o_sb, o_fox = _attention(qkvh, c_rows, sb_g, fox_g, batch, seq)
    return _outproj(o_sb, o_fox, w_o.astype(BF16), x, next_gain)


@jax.jit
def kernel(x, norm_ffn1_g, ffn1_w_gate, ffn1_w_up, ffn1_w_down, norm_mix_g, w_in, b_f,
           sb_out_g, fox_out_g, w_o, norm_ffn2_g, ffn2_w_gate, ffn2_w_up, ffn2_w_down,
           norm_final_g):
    batch, seq, d = x.shape
    depth = w_in.shape[0]
    xf = x.reshape(batch * seq, d)
    h = Normed(_rmsnorm(xf, norm_ffn1_g[0], BF16), None)
    for l in range(depth):
        xf, h = _ffn(xf, h, ffn1_w_gate[l], ffn1_w_up[l], ffn1_w_down[l], next_gain=norm_mix_g[l])
        xf, h = _mixer(xf, h, w_in[l], b_f[l], sb_out_g[l], fox_out_g[l], w_o[l], norm_ffn2_g[l], batch, seq)
        if l + 1 < depth:
            xf, h = _ffn(xf, h, ffn2_w_gate[l], ffn2_w_up[l], ffn2_w_down[l], next_gain=norm_ffn1_g[l + 1])
        else:
            xf = _ffn(xf, h, ffn2_w_gate[l], ffn2_w_up[l], ffn2_w_down[l])
    out = _rmsnorm(xf, norm_final_g, F32)
    return out.reshape(batch, seq, d)
```

```python
import functools
import math
from typing import NamedTuple, Optional

import jax
import jax.numpy as jnp
from jax import lax
from jax.experimental import pallas as pl
from jax.experimental.pallas import tpu as pltpu

HEAD_DIM = 128
N_HEADS = 16
GROUP_WIDTH = N_HEADS * HEAD_DIM
QKV_WIDTH = 6 * GROUP_WIDTH
EPS = 1e-6
FFN_RESIDUAL_SCALE = 0.5
ATTN_SCALE = HEAD_DIM ** -0.5
LOG2E = math.log2(math.e)
NEG_BIG = -1e30
F32_EXP2_UNDERFLOW = -152.0
FOX_SKIP_BELOW = -160.0
NORM_BOUND_MARGIN = 1.01
FOX_DIAGONAL_BANDS = 4

LANES = 128
VMEM_LIMIT_BYTES = 58 * 1024 * 1024

F32 = jnp.float32
BF16 = jnp.bfloat16


def _params(n_axes):
    return pltpu.CompilerParams(
        dimension_semantics=("arbitrary",) * n_axes,
        vmem_limit_bytes=VMEM_LIMIT_BYTES,
    )


def _dot_nt(a, b):
    return lax.dot_general(a, b, (((1,), (1,)), ((), ())), preferred_element_type=F32)


def _rmsnorm_kernel(x_ref, g_ref, o_ref):
    x = x_ref[...]
    y = x * lax.rsqrt(jnp.mean(x * x, axis=-1, keepdims=True) + EPS)
    o_ref[...] = (y * g_ref[...]).astype(o_ref.dtype)


def _rmsnorm(x, g, out_dtype, tm=256):
    m, d = x.shape
    return pl.pallas_call(
        _rmsnorm_kernel,
        grid=(m // tm,),
        in_specs=[pl.BlockSpec((tm, d), lambda i: (i, 0)),
                  pl.BlockSpec((1, d), lambda i: (0, 0))],
        out_specs=pl.BlockSpec((tm, d), lambda i: (i, 0)),
        out_shape=jax.ShapeDtypeStruct((m, d), out_dtype),
        compiler_params=_params(1),
        name="rmsnorm",
    )(x, g.reshape(1, d))


class Normed(NamedTuple):
    values: jax.Array
    sumsq: Optional[jax.Array]


def _emit_normed(y, gain_ref, values_ref, sumsq_ref, col_tile):
    values_ref[...] = (y * gain_ref[...]).astype(BF16)
    part = jnp.broadcast_to(jnp.sum(y * y, axis=-1, keepdims=True), sumsq_ref.shape)

    @pl.when(col_tile == 0)
    def _():
        sumsq_ref[...] = part

    @pl.when(col_tile > 0)
    def _():
        sumsq_ref[...] += part


def _rstd(sumsq, d):
    return lax.rsqrt(sumsq[:, :1] * (1.0 / d) + EPS)


def _normed_specs(tm, d, row_index):
    return [pl.BlockSpec((tm, d), lambda *g: (row_index(*g), 0)),
            pl.BlockSpec((tm, LANES), lambda *g: (row_index(*g), 0))]


def _gateup_kernel(*refs, tf, tail, deferred):
    if deferred:
        h_ref, ss_ref, wg_top, wg_bot, wu_top, wu_bot, o_ref = refs
    else:
        h_ref, wg_top, wg_bot, wu_top, wu_bot, o_ref = refs
    j = pl.program_id(0)
    nj = pl.num_programs(0)
    half = wg_top.shape[0]

    def step(width):
        h_top = h_ref[:, :half]
        h_bot = h_ref[:, half:]

        def project(top, bot):
            return (jnp.dot(h_top, top[:, :width].astype(BF16), preferred_element_type=F32)
                    + jnp.dot(h_bot, bot[:, :width].astype(BF16), preferred_element_type=F32))

        gate = project(wg_top, wg_bot)
        up = project(wu_top, wu_bot)
        if deferred:
            rstd = _rstd(ss_ref[...], h_ref.shape[1])
            gate = gate * rstd
            up = up * rstd
        o_ref[:, :width] = (gate * (1.0 / (1.0 + jnp.exp(-gate))) * up).astype(o_ref.dtype)

    if tail == tf:
        step(tf)
    else:
        pl.when(j < nj - 1)(lambda: step(tf))
        pl.when(j == nj - 1)(lambda: step(tail))


def _gateup(h, w_gate, w_up, tm=1024, tf=512):
    m, d = h.values.shape
    f = w_gate.shape[1]
    nj = pl.cdiv(f, tf)
    tail = f - (nj - 1) * tf
    deferred = h.sumsq is not None
    h_specs = _normed_specs(tm, d, lambda j, i: i)[:2 if deferred else 1]
    w_specs = [pl.BlockSpec((d // 2, tf), lambda j, i: (0, j)),
               pl.BlockSpec((d // 2, tf), lambda j, i: (1, j), pipeline_mode=pl.Buffered(1))]
    return pl.pallas_call(
        functools.partial(_gateup_kernel, tf=tf, tail=tail, deferred=deferred),
        grid=(nj, m // tm),
        in_specs=[*h_specs, *w_specs, *w_specs],
        out_specs=pl.BlockSpec((tm, tf), lambda j, i: (i, j)),
        out_shape=jax.ShapeDtypeStruct((m, f), BF16),
        compiler_params=_params(2),
        name="ffn_gateup",
    )(*(h if deferred else h[:1]), w_gate, w_gate, w_up, w_up)


def _down_kernel(*refs, nk, tail, next_norm):
    if next_norm:
        a_ref, w_ref, x_ref, gain_ref, o_ref, values_ref, sumsq_ref, acc_ref = refs
    else:
        a_ref, w_ref, x_ref, o_ref, acc_ref = refs
    k = pl.program_id(2)

    @pl.when(k == 0)
    def _():
        acc_ref[...] = jnp.zeros_like(acc_ref)

    @pl.when(k < nk - 1)
    def _():
        acc_ref[...] += jnp.dot(a_ref[...], w_ref[...], preferred_element_type=F32)

    @pl.when(k == nk - 1)
    def _():
        last = jnp.dot(a_ref[:, :tail], w_ref[:tail, :], preferred_element_type=F32)
        y = x_ref[...] + FFN_RESIDUAL_SCALE * (acc_ref[...] + last)
        o_ref[...] = y
        if next_norm:
            _emit_normed(y, gain_ref, values_ref, sumsq_ref, pl.program_id(1))


def _down(a, w_d, x, next_gain=None, tm=1024, tn=1024, tk=2816):
    m, f = a.shape
    n = w_d.shape[1]
    nk = pl.cdiv(f, tk)
    tail = f - (nk - 1) * tk
    next_norm = next_gain is not None
    tile = pl.BlockSpec((tm, tn), lambda i, j, k: (i, j))
    in_specs = [pl.BlockSpec((tm, tk), lambda i, j, k: (i, k)),
                pl.BlockSpec((tk, tn), lambda i, j, k: (k, j)), tile]
    out_specs, out_shape, operands = tile, jax.ShapeDtypeStruct((m, n), F32), [a, w_d, x]
    if next_norm:
        in_specs.append(pl.BlockSpec((1, tn), lambda i, j, k: (0, j)))
        operands.append(next_gain.reshape(1, n))
        out_specs = [tile, tile, pl.BlockSpec((tm, LANES), lambda i, j, k: (i, 0))]
        out_shape = [out_shape, jax.ShapeDtypeStruct((m, n), BF16), jax.ShapeDtypeStruct((m, LANES), F32)]
    out = pl.pallas_call(
        functools.partial(_down_kernel, nk=nk, tail=tail, next_norm=next_norm),
        grid=(m // tm, n // tn, nk),
        in_specs=in_specs,
        out_specs=out_specs,
        out_shape=out_shape,
        scratch_shapes=[pltpu.VMEM((tm, tn), F32)],
        compiler_params=_params(3),
        name="ffn_down",
    )(*operands)
    return (out[0], Normed(out[1], out[2])) if next_norm else out


def _ffn(x, h, w_gate, w_up, w_down, next_gain=None):
    a = _gateup(h, w_gate, w_up)
    return _down(a, w_down.astype(BF16), x, next_gain)


def _inproj_kernel(h_ref, ss_ref, wt_left, wt_right, o_ref, *, heads_per_block):
    half = wt_left.shape[1]
    r = (_dot_nt(h_ref[:, :half], wt_left[...].astype(BF16))
         + _dot_nt(h_ref[:, half:], wt_right[...].astype(BF16))) * _rstd(ss_ref[...], h_ref.shape[1])
    for c in range(heads_per_block):
        o_ref[c] = r[:, c * HEAD_DIM:(c + 1) * HEAD_DIM].astype(o_ref.dtype)


def _inproj(h, w_in_t, tm=1024, tn=1024):
    m, d = h.values.shape
    hpb = tn // HEAD_DIM
    return pl.pallas_call(
        functools.partial(_inproj_kernel, heads_per_block=hpb),
        grid=(QKV_WIDTH // tn, m // tm),
        in_specs=[*_normed_specs(tm, d, lambda j, i: i),
                  pl.BlockSpec((tn, d // 2), lambda j, i: (j, 0)),
                  pl.BlockSpec((tn, d // 2), lambda j, i: (j, 1), pipeline_mode=pl.Buffered(1))],
        out_specs=pl.BlockSpec((hpb, tm, HEAD_DIM), lambda j, i: (j, i, 0)),
        out_shape=jax.ShapeDtypeStruct((QKV_WIDTH // HEAD_DIM, m, HEAD_DIM), BF16),
        compiler_params=_params(2),
        name="in_proj",
    )(*h, w_in_t, w_in_t)


def _split3(x):
    hi = x.astype(BF16)
    r = x - hi.astype(F32)
    mid = r.astype(BF16)
    lo = (r - mid.astype(F32)).astype(BF16)
    return hi, mid, lo


def _forget_kernel(h_ref, ss_ref, wt_ref, b_ref, o_ref, wt_bf, carry_ref, *, ts, n_valid):
    s = pl.program_id(1)
    valid = lax.broadcasted_iota(jnp.int32, (LANES, 1), 0) < n_valid

    @pl.when(jnp.logical_and(pl.program_id(0) == 0, s == 0))
    def _():
        wt_bf[...] = jnp.where(valid, wt_ref[...], 0.0).astype(BF16)

    @pl.when(s == 0)
    def _():
        carry_ref[...] = jnp.zeros_like(carry_ref)

    rstd_row = lax.rsqrt(ss_ref[...].T[:1, :] * (1.0 / h_ref.shape[1]) + EPS)
    logit = _dot_nt(wt_bf[...], h_ref[...]) * rstd_row + b_ref[...]
    log_f = jnp.minimum(logit, 0.0) - jnp.log1p(jnp.exp(-jnp.abs(logit)))
    r = lax.broadcasted_iota(jnp.int32, (ts, ts), 0)
    c = lax.broadcasted_iota(jnp.int32, (ts, ts), 1)
    tri = (r <= c).astype(BF16)
    hi, mid, lo = _split3(log_f)
    cum = (jnp.dot(hi, tri, preferred_element_type=F32)
           + jnp.dot(mid, tri, preferred_element_type=F32)
           + jnp.dot(lo, tri, preferred_element_type=F32)) + carry_ref[...]
    o_ref[...] = cum
    carry_ref[...] = cum[:, ts - 1:ts]


def _forget_cumsum(h, w_in_t, b_col, batch, seq, ts=512):
    d = h.values.shape[1]
    ns = seq // ts
    n_valid = w_in_t.shape[0] - QKV_WIDTH
    return pl.pallas_call(
        functools.partial(_forget_kernel, ts=ts, n_valid=n_valid),
        grid=(batch, ns),
        in_specs=[*_normed_specs(ts, d, lambda b, s: b * ns + s),
                  pl.BlockSpec((LANES, d), lambda b, s: (QKV_WIDTH // LANES, 0)),
                  pl.BlockSpec((LANES, 1), lambda b, s: (0, 0))],
        out_specs=pl.BlockSpec((None, LANES, ts), lambda b, s: (b, 0, s)),
        out_shape=jax.ShapeDtypeStruct((batch, LANES, seq), F32),
        scratch_shapes=[pltpu.VMEM((LANES, d), BF16), pltpu.VMEM((LANES, 1), F32)],
        compiler_params=_params(2),
        name="forget_cumsum",
    )(*h, w_in_t, b_col)


def _head_rmsnorm(o, gain):
    return o * lax.rsqrt(jnp.mean(o * o, axis=-1, keepdims=True) + EPS) * gain


def _fox_kernel(q_ref, k_ref, v_ref, qall_ref, c_ref, g_ref, o_ref, vaug_ref, first_ref, s_ref, m_ref, acc_ref,
                *, t, tk):
    i = pl.program_id(2)
    seq = c_ref.shape[1]

    def last_block(qi):
        return ((qi + 1) * t + tk - 1) // tk - 1

    @pl.when(i == 0)
    def _():
        vaug_ref[:, :HEAD_DIM] = v_ref[...]
        vaug_ref[:, HEAD_DIM:] = jnp.ones(v_ref.shape, BF16)
        def max_sq_norm(x_ref):
            xf = x_ref[...].astype(F32)
            return jnp.max(jnp.sum(xf * xf, axis=-1, keepdims=True), axis=0, keepdims=True)

        spread = 2.0 * NORM_BOUND_MARGIN * ATTN_SCALE * jnp.sqrt(max_sq_norm(qall_ref) * max_sq_norm(k_ref))
        c_all = c_ref[...]
        for qi in range(seq // t):
            dead = ((c_ref[:, qi * t:qi * t + 1] - c_all) + spread) * LOG2E < FOX_SKIP_BELOW
            n_dead = jnp.sum(dead.astype(F32)).astype(jnp.int32)
            first_ref[qi] = jnp.minimum(n_dead // tk, last_block(qi))

    q = q_ref[...]
    c_q0 = c_ref[:, pl.ds(pl.multiple_of(i * t, t), LANES)][:, 0:1]
    first = first_ref[i]
    last = last_block(i)

    def scores(j):
        start = pl.multiple_of(j * tk, tk)
        bias = (c_q0 - c_ref[:, pl.ds(start, tk)]) * LOG2E
        return _dot_nt(q, k_ref[pl.ds(start, tk), :]) * (ATTN_SCALE * LOG2E) + bias

    def step(j, cur, width):
        def update(r0, r1, cols, masked):
            s = s_ref[cur, r0:r1, :cols]
            if masked:
                row = i * t + r0 + lax.broadcasted_iota(jnp.int32, (r1 - r0, cols), 0)
                col = j * tk + lax.broadcasted_iota(jnp.int32, (r1 - r0, cols), 1)
                s = jnp.where(col <= row, s, NEG_BIG)
            m_old = m_ref[r0:r1]
            m_new = jnp.maximum(m_old, jnp.max(s, axis=-1, keepdims=True))
            p = jnp.exp2(s - m_new).astype(BF16)
            vs = vaug_ref[pl.ds(pl.multiple_of(j * tk, tk), cols), :]
            acc_ref[r0:r1] = (jnp.exp2(m_old - m_new) * acc_ref[r0:r1]
                              + jnp.dot(p, vs, preferred_element_type=F32))
            m_ref[r0:r1] = m_new

        if width is None:
            s_ref[1 - cur] = scores(j + 1)
            update(0, t, tk, False)
        else:
            band = t // FOX_DIAGONAL_BANDS
            for r0 in range(0, t, band):
                update(r0, r0 + band, width - t + r0 + band, True)

    def either_half(j, width):
        pl.when((j - first) % 2 == 0)(lambda: step(j, 0, width))
        pl.when((j - first) % 2 == 1)(lambda: step(j, 1, width))

    m_ref[...] = jnp.full(m_ref.shape, NEG_BIG, F32)
    acc_ref[...] = jnp.zeros(acc_ref.shape, F32)
    s_ref[0] = scores(first)

    @pl.loop(first, last)
    def _(j):
        either_half(j, None)

    reach = (i * t) % tk + t
    for width in range(t, tk + 1, t):
        pl.when(reach == width)(lambda width=width: either_half(last, width))
    acc = acc_ref[...]
    o_ref[...] = _head_rmsnorm(acc[:, :HEAD_DIM] / acc[:, HEAD_DIM:], g_ref[...]).astype(o_ref.dtype)


def _sb_kernel(q_ref, k_ref, v_ref, g_ref, o_ref, rem_ref, acc_ref, *, t, tk, heads):
    i = pl.program_id(2)
    tri = (lax.broadcasted_iota(jnp.int32, (tk, tk), 0)
           > lax.broadcasted_iota(jnp.int32, (tk, tk), 1)).astype(BF16)
    tri2 = jnp.concatenate([tri, tri], axis=0)

    def scores(h, j, r0, r1, keep_fn):
        z = _dot_nt(q_ref[h, r0:r1, :], k_ref[h, pl.ds(pl.multiple_of(j * tk, tk), tk), :]) * (ATTN_SCALE * LOG2E)
        if keep_fn is not None:
            z = jnp.where(keep_fn(j, r0, r1), z, NEG_BIG)
        ls_pos = jnp.minimum(z, 0.0) - jnp.log2(1.0 + jnp.exp2(-jnp.abs(z)))
        log_keep = ls_pos - z
        return ls_pos, log_keep, jnp.sum(log_keep, axis=-1, keepdims=True)

    def suffix_sums(log_keep):
        hi = log_keep.astype(BF16)
        lo = (log_keep - hi.astype(F32)).astype(BF16)
        return jnp.dot(jnp.concatenate([hi, lo], axis=1), tri2, preferred_element_type=F32)

    def weighted_values(h, j, ls_pos, after, rem):
        a = jnp.exp2(ls_pos + after + rem)
        return jnp.dot(a.astype(BF16), v_ref[h, pl.ds(pl.multiple_of(j * tk, tk), tk), :],
                       preferred_element_type=F32)

    def block(h, j, r0, rem):
        ls_pos, log_keep, tot = scores(h, j, r0, t, None)
        return weighted_values(h, j, ls_pos, suffix_sums(log_keep), rem), tot

    def causal(j, r0, r1):
        return (j * tk + lax.broadcasted_iota(jnp.int32, (r1 - r0, tk), 1)
                < i * t + r0 + lax.broadcasted_iota(jnp.int32, (r1 - r0, tk), 0))

    def pad_rows(x, r0, r1):
        parts = [x]
        if r0:
            parts.insert(0, jnp.zeros((r0, x.shape[1]), F32))
        if r1 < t:
            parts.append(jnp.zeros((t - r1, x.shape[1]), F32))
        return jnp.concatenate(parts, axis=0) if len(parts) > 1 else x

    n_diag = t // tk
    first = (i + 1) * n_diag - 1
    extra = first - n_diag
    group = []
    for h in range(heads):
        for d in range(n_diag):
            group.append((h, first - d, t - (d + 1) * tk, t, causal))
        group.append((h, jnp.maximum(extra, 0), 0, tk, lambda j, r0, r1: extra >= 0))
    scored = [scores(*blk) for blk in group]
    afters = [suffix_sums(log_keep) for _, log_keep, _ in scored]
    rem = [jnp.zeros((t, 1), F32) for _ in range(heads)]
    rest_alive = [False] * heads
    outs = []
    for (h, j, r0, r1, _), (ls_pos, _, tot), after in zip(group, scored, afters):
        if r1 < t:
            rest_alive[h] = jnp.max(rem[h][tk:]) > F32_EXP2_UNDERFLOW
        outs.append(weighted_values(h, j, ls_pos, after, rem[h][r0:r1]))
        rem[h] = rem[h] + pad_rows(tot, r0, r1)
    for h in range(heads):
        acc_ref[h] = sum(pad_rows(out, r0, r1) for (hh, _, r0, r1, _), out in zip(group, outs) if hh == h)
        rem_ref[h] = rem[h]
    rem_max = [jnp.max(rem[h]) for h in range(heads)]

    def cond(state):
        j, rem_max = state
        return jnp.logical_and(j >= 0, rem_max > F32_EXP2_UNDERFLOW)

    for h in range(heads):
        if tk < t:
            @pl.when(jnp.logical_and(rest_alive[h], extra >= 0))
            def _(h=h):
                out, tot = block(h, extra, tk, rem_ref[h, tk:])
                acc_ref[h, tk:] += out
                rem_ref[h, tk:] += tot

        def body(state, h=h):
            j, _ = state
            rem = rem_ref[h]
            out, tot = block(h, j, 0, rem)
            acc_ref[h] += out
            rem_ref[h] = rem + tot
            return j - 1, jnp.max(rem + tot)

        lax.while_loop(cond, body, (extra - 1, rem_max[h]))

    for h in range(heads):
        o_ref[:, h * HEAD_DIM:(h + 1) * HEAD_DIM] = _head_rmsnorm(acc_ref[h], g_ref[h]).astype(o_ref.dtype)


def _attention_call(kernel_fn, name, qkvh, first_section, extra_inputs, extra_specs, gain, batch, seq, t,
                    scratch_shapes=(), heads=None):
    nq = seq // t
    per_step = heads or 1
    groups = N_HEADS // per_step

    def section_spec(section, rows):
        blocks_per_batch = seq // rows
        return pl.BlockSpec(
            (heads, rows, HEAD_DIM),
            lambda b, g, i: (section * groups + g, b * blocks_per_batch + (i if rows == t else 0), 0))

    return pl.pallas_call(
        kernel_fn,
        grid=(batch, groups, nq),
        in_specs=[section_spec(first_section, t), section_spec(first_section + 1, seq),
                  section_spec(first_section + 2, seq), *extra_specs,
                  pl.BlockSpec((heads, 1, HEAD_DIM), lambda b, g, i: (g, 0, 0))],
        out_specs=pl.BlockSpec((t, per_step * HEAD_DIM), lambda b, g, i: (b * nq + i, g)),
        out_shape=jax.ShapeDtypeStruct((batch * seq, GROUP_WIDTH), BF16),
        scratch_shapes=list(scratch_shapes),
        compiler_params=_params(3),
        name=name,
    )(qkvh, qkvh, qkvh, *extra_inputs, gain.reshape(N_HEADS, 1, HEAD_DIM))


def _attention(qkvh, c_rows, sb_gain, fox_gain, batch, seq, t_sb=512, tk_sb=256, heads_sb=4,
               t_fox=1024, tk_fox=1024):
    assert t_sb % tk_sb == 0 and seq % t_sb == 0 and N_HEADS % heads_sb == 0
    sb_scratch = [pltpu.VMEM((heads_sb, t_sb, 1), F32), pltpu.VMEM((heads_sb, t_sb, HEAD_DIM), F32)]
    o_sb = _attention_call(functools.partial(_sb_kernel, t=t_sb, tk=tk_sb, heads=heads_sb), "sb_attention",
                           qkvh, 0, (), (), sb_gain, batch, seq, t_sb, sb_scratch, heads=heads_sb)
    assert seq % tk_fox == 0 and tk_fox % t_fox == 0
    fox_q_section = 3
    q_all_spec = pl.BlockSpec((None, seq, HEAD_DIM), lambda b, h, i: (fox_q_section * N_HEADS + h, b, 0))
    c_spec = pl.BlockSpec((None, None, 1, seq), lambda b, h, i: (b, h, 0, 0))
    fox_scratch = [pltpu.VMEM((seq, 2 * HEAD_DIM), BF16),
                   pltpu.SMEM((seq // t_fox,), jnp.int32),
                   pltpu.VMEM((2, t_fox, tk_fox), F32),
                   pltpu.VMEM((t_fox, 1), F32),
                   pltpu.VMEM((t_fox, 2 * HEAD_DIM), F32)]
    o_fox = _attention_call(functools.partial(_fox_kernel, t=t_fox, tk=tk_fox), "fox_attention", qkvh,
                            fox_q_section, (qkvh, c_rows), (q_all_spec, c_spec), fox_gain, batch, seq,
                            t_fox, fox_scratch)
    return o_sb, o_fox


def _outproj_kernel(a_ref, b_ref, w_ref, x_ref, gain_ref, o_ref, values_ref, sumsq_ref):
    acc = jnp.dot(a_ref[...], w_ref[:GROUP_WIDTH, :], preferred_element_type=F32)
    acc = acc + jnp.dot(b_ref[...], w_ref[GROUP_WIDTH:, :], preferred_element_type=F32)
    y = x_ref[...] + acc
    o_ref[...] = y
    _emit_normed(y, gain_ref, values_ref, sumsq_ref, pl.program_id(1))


def _outproj(a, b, w_o, x, next_gain, tm=1024, tn=512):
    m = a.shape[0]
    n = w_o.shape[1]
    tile = pl.BlockSpec((tm, tn), lambda i, j: (i, j))
    y, values, sumsq = pl.pallas_call(
        _outproj_kernel,
        grid=(m // tm, n // tn),
        in_specs=[pl.BlockSpec((tm, GROUP_WIDTH), lambda i, j: (i, 0)),
                  pl.BlockSpec((tm, GROUP_WIDTH), lambda i, j: (i, 0)),
                  pl.BlockSpec((2 * GROUP_WIDTH, tn), lambda i, j: (0, j)),
                  tile,
                  pl.BlockSpec((1, tn), lambda i, j: (0, j))],
        out_specs=[tile, tile, pl.BlockSpec((tm, LANES), lambda i, j: (i, 0))],
        out_shape=[jax.ShapeDtypeStruct((m, n), F32), jax.ShapeDtypeStruct((m, n), BF16),
                   jax.ShapeDtypeStruct((m, LANES), F32)],
        compiler_params=_params(2),
        name="out_proj",
    )(a, b, w_o, x, next_gain.reshape(1, n))
    return y, Normed(values, sumsq)


def _mixer(x, h, w_in, b_f, sb_g, fox_g, w_o, next_gain, batch, seq):
    w_in_t = w_in.T
    qkvh = _inproj(h, w_in_t)
    b_col = jnp.pad(b_f, (0, LANES - N_HEADS)).reshape(LANES, 1)
    c = _forget_cumsum(h, w_in_t, b_col, batch, seq)
    c_rows = c[:, :N_HEADS, :].reshape(batch, N_HEADS, 1, seq)
    o_sb, o_fox = _attention(qkvh, c_rows, sb_g, fox_g, batch, seq)
    return _outproj(o_sb, o_fox, w_o.astype(BF16), x, next_gain)


@jax.jit
def kernel(x, norm_ffn1_g, ffn1_w_gate, ffn1_w_up, ffn1_w_down, norm_mix_g, w_in, b_f,
           sb_out_g, fox_out_g, w_o, norm_ffn2_g, ffn2_w_gate, ffn2_w_up, ffn2_w_down,
           norm_final_g):
    batch, seq, d = x.shape
    depth = w_in.shape[0]
    xf = x.reshape(batch * seq, d)
    h = Normed(_rmsnorm(xf, norm_ffn1_g[0], BF16), None)
    for l in range(depth):
        xf, h = _ffn(xf, h, ffn1_w_gate[l], ffn1_w_up[l], ffn1_w_down[l], next_gain=norm_mix_g[l])
        xf, h = _mixer(xf, h, w_in[l], b_f[l], sb_out_g[l], fox_out_g[l], w_o[l], norm_ffn2_g[l], batch, seq)
        if l + 1 < depth:
            xf, h = _ffn(xf, h, ffn2_w_gate[l], ffn2_w_up[l], ffn2_w_down[l], next_gain=norm_ffn1_g[l + 1])
        else:
            xf = _ffn(xf, h, ffn2_w_gate[l], ffn2_w_up[l], ffn2_w_down[l])
    out = _rmsnorm(xf, norm_final_g, F32)
    return out.reshape(batch, seq, d)
```

```python
import functools
import math
from typing import NamedTuple, Optional

import jax
import jax.numpy as jnp
from jax import lax
from jax.experimental import pallas as pl
from jax.experimental.pallas import tpu as pltpu

HEAD_DIM = 128
N_HEADS = 16
GROUP_WIDTH = N_HEADS * HEAD_DIM
QKV_WIDTH = 6 * GROUP_WIDTH
EPS = 1e-6
FFN_RESIDUAL_SCALE = 0.5
ATTN_SCALE = HEAD_DIM ** -0.5
LOG2E = math.log2(math.e)
NEG_BIG = -1e30
F32_EXP2_UNDERFLOW = -152.0
FOX_SKIP_BELOW = -160.0
NORM_BOUND_MARGIN = 1.01
FOX_DIAGONAL_BANDS = 4

LANES = 128
VMEM_LIMIT_BYTES = 58 * 1024 * 1024

F32 = jnp.float32
BF16 = jnp.bfloat16


def _params(n_axes):
    return pltpu.CompilerParams(
        dimension_semantics=("arbitrary",) * n_axes,
        vmem_limit_bytes=VMEM_LIMIT_BYTES,
    )


def _dot_nt(a, b):
    return lax.dot_general(a, b, (((1,), (1,)), ((), ())), preferred_element_type=F32)


def _neg_abs(x):
    bits = lax.bitcast_convert_type(x, jnp.uint32) | jnp.uint32(0x80000000)
    return lax.bitcast_convert_type(bits, F32)


def _rmsnorm_kernel(x_ref, g_ref, o_ref):
    x = x_ref[...]
    y = x * lax.rsqrt(jnp.mean(x * x, axis=-1, keepdims=True) + EPS)
    o_ref[...] = (y * g_ref[...]).astype(o_ref.dtype)


def _rmsnorm(x, g, out_dtype, tm=512):
    m, d = x.shape
    return pl.pallas_call(
        _rmsnorm_kernel,
        grid=(m // tm,),
        in_specs=[pl.BlockSpec((tm, d), lambda i: (i, 0)),
                  pl.BlockSpec((1, d), lambda i: (0, 0))],
        out_specs=pl.BlockSpec((tm, d), lambda i: (i, 0)),
        out_shape=jax.ShapeDtypeStruct((m, d), out_dtype),
        compiler_params=_params(1),
        name="rmsnorm",
    )(x, g.reshape(1, d))


class Normed(NamedTuple):
    values: jax.Array
    sumsq: Optional[jax.Array]


def _emit_normed(y, gain_ref, values_ref, sumsq_ref, col_tile):
    values_ref[...] = (y * gain_ref[...]).astype(BF16)
    part = jnp.broadcast_to(jnp.sum(y * y, axis=-1, keepdims=True), sumsq_ref.shape)

    @pl.when(col_tile == 0)
    def _():
        sumsq_ref[...] = part

    @pl.when(col_tile > 0)
    def _():
        sumsq_ref[...] += part


def _rstd(sumsq, d):
    return lax.rsqrt(sumsq[:, :1] * (1.0 / d) + EPS)


def _normed_specs(tm, d, row_index):
    return [pl.BlockSpec((tm, d), lambda *g: (row_index(*g), 0)),
            pl.BlockSpec((tm, LANES), lambda *g: (row_index(*g), 0))]


def _gateup_kernel(*refs, tf, tail, deferred):
    if deferred:
        h_ref, ss_ref, wg_top, wg_bot, wu_top, wu_bot, o_ref = refs
    else:
        h_ref, wg_top, wg_bot, wu_top, wu_bot, o_ref = refs
    j = pl.program_id(0)
    nj = pl.num_programs(0)
    half = wg_top.shape[0]

    def step(width):
        h_top = h_ref[:, :half]
        h_bot = h_ref[:, half:]

        def project(top, bot):
            return (jnp.dot(h_top, top[:, :width].astype(BF16), preferred_element_type=F32)
                    + jnp.dot(h_bot, bot[:, :width].astype(BF16), preferred_element_type=F32))

        gate = project(wg_top, wg_bot)
        up = project(wu_top, wu_bot)
        if deferred:
            rstd = _rstd(ss_ref[...], h_ref.shape[1])
            gate = gate * rstd
            up = up * rstd
        o_ref[:, :width] = (gate * (1.0 / (1.0 + jnp.exp(-gate))) * up).astype(o_ref.dtype)

    if tail == tf:
        step(tf)
    else:
        pl.when(j < nj - 1)(lambda: step(tf))
        pl.when(j == nj - 1)(lambda: step(tail))


def _gateup(h, w_gate, w_up, tm=1024, tf=512):
    m, d = h.values.shape
    f = w_gate.shape[1]
    nj = pl.cdiv(f, tf)
    tail = f - (nj - 1) * tf
    deferred = h.sumsq is not None
    h_specs = _normed_specs(tm, d, lambda j, i: i)[:2 if deferred else 1]
    w_specs = [pl.BlockSpec((d // 2, tf), lambda j, i: (0, j)),
               pl.BlockSpec((d // 2, tf), lambda j, i: (1, j), pipeline_mode=pl.Buffered(1))]
    return pl.pallas_call(
        functools.partial(_gateup_kernel, tf=tf, tail=tail, deferred=deferred),
        grid=(nj, m // tm),
        in_specs=[*h_specs, *w_specs, *w_specs],
        out_specs=pl.BlockSpec((tm, tf), lambda j, i: (i, j)),
        out_shape=jax.ShapeDtypeStruct((m, f), BF16),
        compiler_params=_params(2),
        name="ffn_gateup",
    )(*(h if deferred else h[:1]), w_gate, w_gate, w_up, w_up)


def _down_kernel(*refs, nk, tail, next_norm):
    if next_norm:
        a_ref, w_ref, x_ref, gain_ref, o_ref, values_ref, sumsq_ref, acc_ref = refs
    else:
        a_ref, w_ref, x_ref, o_ref, acc_ref = refs
    k = pl.program_id(2)

    @pl.when(k == 0)
    def _():
        acc_ref[...] = jnp.zeros_like(acc_ref)

    @pl.when(k < nk - 1)
    def _():
        acc_ref[...] += jnp.dot(a_ref[...], w_ref[...], preferred_element_type=F32)

    @pl.when(k == nk - 1)
    def _():
        last = jnp.dot(a_ref[:, :tail], w_ref[:tail, :], preferred_element_type=F32)
        y = x_ref[...] + FFN_RESIDUAL_SCALE * (acc_ref[...] + last)
        o_ref[...] = y
        if next_norm:
            _emit_normed(y, gain_ref, values_ref, sumsq_ref, pl.program_id(1))


def _down(a, w_d, x, next_gain=None, tm=1024, tn=1024, tk=2816):
    m, f = a.shape
    n = w_d.shape[1]
    nk = pl.cdiv(f, tk)
    tail = f - (nk - 1) * tk
    next_norm = next_gain is not None
    tile = pl.BlockSpec((tm, tn), lambda i, j, k: (i, j))
    in_specs = [pl.BlockSpec((tm, tk), lambda i, j, k: (i, k)),
                pl.BlockSpec((tk, tn), lambda i, j, k: (k, j)), tile]
    out_specs, out_shape, operands = tile, jax.ShapeDtypeStruct((m, n), F32), [a, w_d, x]
    if next_norm:
        in_specs.append(pl.BlockSpec((1, tn), lambda i, j, k: (0, j)))
        operands.append(next_gain.reshape(1, n))
        out_specs = [tile, tile, pl.BlockSpec((tm, LANES), lambda i, j, k: (i, 0))]
        out_shape = [out_shape, jax.ShapeDtypeStruct((m, n), BF16), jax.ShapeDtypeStruct((m, LANES), F32)]
    out = pl.pallas_call(
        functools.partial(_down_kernel, nk=nk, tail=tail, next_norm=next_norm),
        grid=(m // tm, n // tn, nk),
        in_specs=in_specs,
        out_specs=out_specs,
        out_shape=out_shape,
        scratch_shapes=[pltpu.VMEM((tm, tn), F32)],
        compiler_params=_params(3),
        name="ffn_down",
    )(*operands)
    return (out[0], Normed(out[1], out[2])) if next_norm else out


def _ffn(x, h, w_gate, w_up, w_down, next_gain=None):
    a = _gateup(h, w_gate, w_up)
    return _down(a, w_down.astype(BF16), x, next_gain)


def _inproj_kernel(h_ref, ss_ref, wt_left, wt_right, o_ref, *, heads_per_block):
    half = wt_left.shape[1]
    r = (_dot_nt(h_ref[:, :half], wt_left[...].astype(BF16))
         + _dot_nt(h_ref[:, half:], wt_right[...].astype(BF16))) * _rstd(ss_ref[...], h_ref.shape[1])
    for c in range(heads_per_block):
        o_ref[c] = r[:, c * HEAD_DIM:(c + 1) * HEAD_DIM].astype(o_ref.dtype)


def _inproj(h, w_in_t, tm=1024, tn=1024):
    m, d = h.values.shape
    hpb = tn // HEAD_DIM
    return pl.pallas_call(
        functools.partial(_inproj_kernel, heads_per_block=hpb),
        grid=(QKV_WIDTH // tn, m // tm),
        in_specs=[*_normed_specs(tm, d, lambda j, i: i),
                  pl.BlockSpec((tn, d // 2), lambda j, i: (j, 0)),
                  pl.BlockSpec((tn, d // 2), lambda j, i: (j, 1), pipeline_mode=pl.Buffered(1))],
        out_specs=pl.BlockSpec((hpb, tm, HEAD_DIM), lambda j, i: (j, i, 0)),
        out_shape=jax.ShapeDtypeStruct((QKV_WIDTH // HEAD_DIM, m, HEAD_DIM), BF16),
        compiler_params=_params(2),
        name="in_proj",
    )(*h, w_in_t, w_in_t)


def _split3(x):
    hi = x.astype(BF16)
    r = x - hi.astype(F32)
    mid = r.astype(BF16)
    lo = (r - mid.astype(F32)).astype(BF16)
    return hi, mid, lo


def _forget_kernel(h_ref, ss_ref, wt_ref, b_ref, o_ref, wt_bf, carry_ref, *, ts, n_valid):
    s = pl.program_id(1)
    valid = lax.broadcasted_iota(jnp.int32, (LANES, 1), 0) < n_valid

    @pl.when(jnp.logical_and(pl.program_id(0) == 0, s == 0))
    def _():
        wt_bf[...] = jnp.where(valid, wt_ref[...], 0.0).astype(BF16)

    @pl.when(s == 0)
    def _():
        carry_ref[...] = jnp.zeros_like(carry_ref)

    rstd_row = lax.rsqrt(ss_ref[...].T[:1, :] * (1.0 / h_ref.shape[1]) + EPS)
    logit = _dot_nt(wt_bf[...], h_ref[...]) * rstd_row + b_ref[...]
    log_f = jnp.minimum(logit, 0.0) - jnp.log1p(jnp.exp(-jnp.abs(logit)))
    r = lax.broadcasted_iota(jnp.int32, (ts, ts), 0)
    c = lax.broadcasted_iota(jnp.int32, (ts, ts), 1)
    tri = (r <= c).astype(BF16)
    hi, mid, lo = _split3(log_f)
    cum = (jnp.dot(hi, tri, preferred_element_type=F32)
           + jnp.dot(mid, tri, preferred_element_type=F32)
           + jnp.dot(lo, tri, preferred_element_type=F32)) + carry_ref[...]
    o_ref[...] = cum
    carry_ref[...] = cum[:, ts - 1:ts]


def _forget_cumsum(h, w_in_t, b_col, batch, seq, ts=512):
    d = h.values.shape[1]
    ns = seq // ts
    n_valid = w_in_t.shape[0] - QKV_WIDTH
    return pl.pallas_call(
        functools.partial(_forget_kernel, ts=ts, n_valid=n_valid),
        grid=(batch, ns),
        in_specs=[*_normed_specs(ts, d, lambda b, s: b * ns + s),
                  pl.BlockSpec((LANES, d), lambda b, s: (QKV_WIDTH // LANES, 0)),
                  pl.BlockSpec((LANES, 1), lambda b, s: (0, 0))],
        out_specs=pl.BlockSpec((None, LANES, ts), lambda b, s: (b, 0, s)),
        out_shape=jax.ShapeDtypeStruct((batch, LANES, seq), F32),
        scratch_shapes=[pltpu.VMEM((LANES, d), BF16), pltpu.VMEM((LANES, 1), F32)],
        compiler_params=_params(2),
        name="forget_cumsum",
    )(*h, w_in_t, b_col)


def _head_rmsnorm(o, gain):
    return o * lax.rsqrt(jnp.mean(o * o, axis=-1, keepdims=True) + EPS) * gain


def _fox_kernel(q_ref, k_ref, v_ref, qall_ref, c_ref, g_ref, o_ref, vaug_ref, first_ref, s_ref, m_ref, acc_ref,
                *, t, tk):
    i = pl.program_id(2)
    seq = c_ref.shape[1]

    def last_block(qi):
        return ((qi + 1) * t + tk - 1) // tk - 1

    @pl.when(i == 0)
    def _():
        vaug_ref[:, :HEAD_DIM] = v_ref[...]
        vaug_ref[:, HEAD_DIM:] = jnp.ones(v_ref.shape, BF16)
        def max_sq_norm(x_ref):
            xf = x_ref[...].astype(F32)
            return jnp.max(jnp.sum(xf * xf, axis=-1, keepdims=True), axis=0, keepdims=True)

        spread = 2.0 * NORM_BOUND_MARGIN * ATTN_SCALE * jnp.sqrt(max_sq_norm(qall_ref) * max_sq_norm(k_ref))
        c_all = c_ref[...]
        for qi in range(seq // t):
            dead = ((c_ref[:, qi * t:qi * t + 1] - c_all) + spread) * LOG2E < FOX_SKIP_BELOW
            n_dead = jnp.sum(dead.astype(F32)).astype(jnp.int32)
            first_ref[qi] = jnp.minimum(n_dead // tk, last_block(qi))

    q = q_ref[...]
    c_q0 = c_ref[:, pl.ds(pl.multiple_of(i * t, t), LANES)][:, 0:1]
    first = first_ref[i]
    last = last_block(i)

    def scores(j):
        start = pl.multiple_of(j * tk, tk)
        bias = (c_q0 - c_ref[:, pl.ds(start, tk)]) * LOG2E
        return _dot_nt(q, k_ref[pl.ds(start, tk), :]) * (ATTN_SCALE * LOG2E) + bias

    def step(j, cur, width):
        def update(r0, r1, cols, masked):
            s = s_ref[cur, r0:r1, :cols]
            if masked:
                row = i * t + r0 + lax.broadcasted_iota(jnp.int32, (r1 - r0, cols), 0)
                col = j * tk + lax.broadcasted_iota(jnp.int32, (r1 - r0, cols), 1)
                s = jnp.where(col <= row, s, NEG_BIG)
            m_old = m_ref[r0:r1]
            m_new = jnp.maximum(m_old, jnp.max(s, axis=-1, keepdims=True))
            p = jnp.exp2(s - m_new).astype(BF16)
            vs = vaug_ref[pl.ds(pl.multiple_of(j * tk, tk), cols), :]
            acc_ref[r0:r1] = (jnp.exp2(m_old - m_new) * acc_ref[r0:r1]
                              + jnp.dot(p, vs, preferred_element_type=F32))
            m_ref[r0:r1] = m_new

        if width is None:
            s_ref[1 - cur] = scores(j + 1)
            update(0, t, tk, False)
        else:
            band = t // FOX_DIAGONAL_BANDS
            for r0 in range(0, t, band):
                update(r0, r0 + band, width - t + r0 + band, True)

    def either_half(j, width):
        pl.when((j - first) % 2 == 0)(lambda: step(j, 0, width))
        pl.when((j - first) % 2 == 1)(lambda: step(j, 1, width))

    m_ref[...] = jnp.full(m_ref.shape, NEG_BIG, F32)
    acc_ref[...] = jnp.zeros(acc_ref.shape, F32)
    s_ref[0] = scores(first)

    @pl.loop(first, last)
    def _(j):
        either_half(j, None)

    reach = (i * t) % tk + t
    for width in range(t, tk + 1, t):
        pl.when(reach == width)(lambda width=width: either_half(last, width))
    acc = acc_ref[...]
    o_ref[...] = _head_rmsnorm(acc[:, :HEAD_DIM] / acc[:, HEAD_DIM:], g_ref[...]).astype(o_ref.dtype)


def _sb_kernel(q_ref, k_ref, v_ref, g_ref, o_ref, rem_ref, acc_ref, *, t, tk, heads):
    i = pl.program_id(2)
    tri = (lax.broadcasted_iota(jnp.int32, (tk, tk), 0)
           > lax.broadcasted_iota(jnp.int32, (tk, tk), 1)).astype(BF16)
    tri2 = jnp.concatenate([tri, tri], axis=0)

    def scores(h, j, r0, r1, keep_fn):
        z = _dot_nt(q_ref[h, r0:r1, :], k_ref[h, pl.ds(pl.multiple_of(j * tk, tk), tk), :]) * (ATTN_SCALE * LOG2E)
        if keep_fn is not None:
            z = jnp.where(keep_fn(j, r0, r1), z, NEG_BIG)
        ls_pos = jnp.minimum(z, 0.0) - jnp.log2(1.0 + jnp.exp2(_neg_abs(z)))
        log_keep = ls_pos - z
        return ls_pos, log_keep, jnp.sum(log_keep, axis=-1, keepdims=True)

    def suffix_sums(log_keep):
        hi = log_keep.astype(BF16)
        lo = (log_keep - hi.astype(F32)).astype(BF16)
        return jnp.dot(jnp.concatenate([hi, lo], axis=1), tri2, preferred_element_type=F32)

    def weighted_values(h, j, ls_pos, after, rem):
        a = jnp.exp2(ls_pos + after + rem)
        return jnp.dot(a.astype(BF16), v_ref[h, pl.ds(pl.multiple_of(j * tk, tk), tk), :],
                       preferred_element_type=F32)

    def block(h, j, r0, rem):
        ls_pos, log_keep, tot = scores(h, j, r0, t, None)
        return weighted_values(h, j, ls_pos, suffix_sums(log_keep), rem), tot

    def causal(j, r0, r1):
        return (j * tk + lax.broadcasted_iota(jnp.int32, (r1 - r0, tk), 1)
                < i * t + r0 + lax.broadcasted_iota(jnp.int32, (r1 - r0, tk), 0))

    def pad_rows(x, r0, r1):
        parts = [x]
        if r0:
            parts.insert(0, jnp.zeros((r0, x.shape[1]), F32))
        if r1 < t:
            parts.append(jnp.zeros((t - r1, x.shape[1]), F32))
        return jnp.concatenate(parts, axis=0) if len(parts) > 1 else x

    n_diag = t // tk
    first = (i + 1) * n_diag - 1
    extra = first - n_diag
    group = []
    for h in range(heads):
        for d in range(n_diag):
            group.append((h, first - d, t - (d + 1) * tk, t, causal))
        group.append((h, jnp.maximum(extra, 0), 0, tk, lambda j, r0, r1: extra >= 0))
    scored = [scores(*blk) for blk in group]
    afters = [suffix_sums(log_keep) for _, log_keep, _ in scored]
    rem = [jnp.zeros((t, 1), F32) for _ in range(heads)]
    rest_alive = [False] * heads
    outs = []
    for (h, j, r0, r1, _), (ls_pos, _, tot), after in zip(group, scored, afters):
        if r1 < t:
            rest_alive[h] = jnp.max(rem[h][tk:]) > F32_EXP2_UNDERFLOW
        outs.append(weighted_values(h, j, ls_pos, after, rem[h][r0:r1]))
        rem[h] = rem[h] + pad_rows(tot, r0, r1)
    for h in range(heads):
        acc_ref[h] = sum(pad_rows(out, r0, r1) for (hh, _, r0, r1, _), out in zip(group, outs) if hh == h)
        rem_ref[h] = rem[h]
    rem_max = [jnp.max(rem[h]) for h in range(heads)]

    def cond(state):
        j, rem_max = state
        return jnp.logical_and(j >= 0, rem_max > F32_EXP2_UNDERFLOW)

    for h in range(heads):
        if tk < t:
            @pl.when(jnp.logical_and(rest_alive[h], extra >= 0))
            def _(h=h):
                out, tot = block(h, extra, tk, rem_ref[h, tk:])
                acc_ref[h, tk:] += out
                rem_ref[h, tk:] += tot

        def body(state, h=h):
            j, _ = state
            rem = rem_ref[h]
            out, tot = block(h, j, 0, rem)
            acc_ref[h] += out
            rem_ref[h] = rem + tot
            return j - 1, jnp.max(rem + tot)

        lax.while_loop(cond, body, (extra - 1, rem_max[h]))

    for h in range(heads):
        o_ref[:, h * HEAD_DIM:(h + 1) * HEAD_DIM] = _head_rmsnorm(acc_ref[h], g_ref[h]).astype(o_ref.dtype)


def _attention_call(kernel_fn, name, qkvh, first_section, extra_inputs, extra_specs, gain, batch, seq, t,
                    scratch_shapes=(), heads=None):
    nq = seq // t
    per_step = heads or 1
    groups = N_HEADS // per_step

    def section_spec(section, rows):
        blocks_per_batch = seq // rows
        return pl.BlockSpec(
            (heads, rows, HEAD_DIM),
            lambda b, g, i: (section * groups + g, b * blocks_per_batch + (i if rows == t else 0), 0))

    return pl.pallas_call(
        kernel_fn,
        grid=(batch, groups, nq),
        in_specs=[section_spec(first_section, t), section_spec(first_section + 1, seq),
                  section_spec(first_section + 2, seq), *extra_specs,
                  pl.BlockSpec((heads, 1, HEAD_DIM), lambda b, g, i: (g, 0, 0))],
        out_specs=pl.BlockSpec((t, per_step * HEAD_DIM), lambda b, g, i: (b * nq + i, g)),
        out_shape=jax.ShapeDtypeStruct((batch * seq, GROUP_WIDTH), BF16),
        scratch_shapes=list(scratch_shapes),
        compiler_params=_params(3),
        name=name,
    )(qkvh, qkvh, qkvh, *extra_inputs, gain.reshape(N_HEADS, 1, HEAD_DIM))


def _attention(qkvh, c_rows, sb_gain, fox_gain, batch, seq, t_sb=512, tk_sb=256, heads_sb=4,
               t_fox=1024, tk_fox=1024):
    assert t_sb % tk_sb == 0 and seq % t_sb == 0 and N_HEADS % heads_sb == 0
    sb_scratch = [pltpu.VMEM((heads_sb, t_sb, 1), F32), pltpu.VMEM((heads_sb, t_sb, HEAD_DIM), F32)]
    o_sb = _attention_call(functools.partial(_sb_kernel, t=t_sb, tk=tk_sb, heads=heads_sb), "sb_attention",
                           qkvh, 0, (), (), sb_gain, batch, seq, t_sb, sb_scratch, heads=heads_sb)
    assert seq % tk_fox == 0 and tk_fox % t_fox == 0
    fox_q_section = 3
    q_all_spec = pl.BlockSpec((None, seq, HEAD_DIM), lambda b, h, i: (fox_q_section * N_HEADS + h, b, 0))
    c_spec = pl.BlockSpec((None, None, 1, seq), lambda b, h, i: (b, h, 0, 0))
    fox_scratch = [pltpu.VMEM((seq, 2 * HEAD_DIM), BF16),
                   pltpu.SMEM((seq // t_fox,), jnp.int32),
                   pltpu.VMEM((2, t_fox, tk_fox), F32),
                   pltpu.VMEM((t_fox, 1), F32),
                   pltpu.VMEM((t_fox, 2 * HEAD_DIM), F32)]
    o_fox = _attention_call(functools.partial(_fox_kernel, t=t_fox, tk=tk_fox), "fox_attention", qkvh,
                            fox_q_section, (qkvh, c_rows), (q_all_spec, c_spec), fox_gain, batch, seq,
                            t_fox, fox_scratch)
    return o_sb, o_fox


def _outproj_kernel(a_ref, b_ref, w_ref, x_ref, gain_ref, o_ref, values_ref, sumsq_ref):
    acc = jnp.dot(a_ref[...], w_ref[:GROUP_WIDTH, :], preferred_element_type=F32)
    acc = acc + jnp.dot(b_ref[...], w_ref[GROUP_WIDTH:, :], preferred_element_type=F32)
    y = x_ref[...] + acc
    o_ref[...] = y
    _emit_normed(y, gain_ref, values_ref, sumsq_ref, pl.program_id(1))


def _outproj(a, b, w_o, x, next_gain, tm=1024, tn=512):
    m = a.shape[0]
    n = w_o.shape[1]
    tile = pl.BlockSpec((tm, tn), lambda i, j: (i, j))
    y, values, sumsq = pl.pallas_call(
        _outproj_kernel,
        grid=(m // tm, n // tn),
        in_specs=[pl.BlockSpec((tm, GROUP_WIDTH), lambda i, j: (i, 0)),
                  pl.BlockSpec((tm, GROUP_WIDTH), lambda i, j: (i, 0)),
                  pl.BlockSpec((2 * GROUP_WIDTH, tn), lambda i, j: (0, j)),
                  tile,
                  pl.BlockSpec((1, tn), lambda i, j: (0, j))],
        out_specs=[tile, tile, pl.BlockSpec((tm, LANES), lambda i, j: (i, 0))],
        out_shape=[jax.ShapeDtypeStruct((m, n), F32), jax.ShapeDtypeStruct((m, n), BF16),
                   jax.ShapeDtypeStruct((m, LANES), F32)],
        compiler_params=_params(2),
        name="out_proj",
    )(a, b, w_o, x, next_gain.reshape(1, n))
    return y, Normed(values, sumsq)


def _mixer(x, h, w_in, b_f, sb_g, fox_g, w_o, next_gain, batch, seq):
    w_in_t = w_in.T
    qkvh = _inproj(h, w_in_t)
    b_col = jnp.pad(b_f, (0, LANES - N_HEADS)).reshape(LANES, 1)
    c = _forget_cumsum(h, w_in_t, b_col, batch, seq)
    c_rows = c[:, :N_HEADS, :].reshape(batch, N_HEADS, 1, seq)
    o_sb, o_fox = _attention(qkvh, c_rows, sb_g, fox_g, batch, seq)
    return _outproj(o_sb, o_fox, w_o.astype(BF16), x, next_gain)


@jax.jit
def kernel(x, norm_ffn1_g, ffn1_w_gate, ffn1_w_up, ffn1_w_down, norm_mix_g, w_in, b_f,
           sb_out_g, fox_out_g, w_o, norm_ffn2_g, ffn2_w_gate, ffn2_w_up, ffn2_w_down,
           norm_final_g):
    batch, seq, d = x.shape
    depth = w_in.shape[0]
    xf = x.reshape(batch * seq, d)
    h = Normed(_rmsnorm(xf, norm_ffn1_g[0], BF16), None)
    for l in range(depth):
        xf, h = _ffn(xf, h, ffn1_w_gate[l], ffn1_w_up[l], ffn1_w_down[l], next_gain=norm_mix_g[l])
        xf, h = _mixer(xf, h, w_in[l], b_f[l], sb_out_g[l], fox_out_g[l], w_o[l], norm_ffn2_g[l], batch, seq)
        if l + 1 < depth:
            xf, h = _ffn(xf, h, ffn2_w_gate[l], ffn2_w_up[l], ffn2_w_down[l], next_gain=norm_ffn1_g[l + 1])
        else:
            xf = _ffn(xf, h, ffn2_w_gate[l], ffn2_w_up[l], ffn2_w_down[l])
    out = _rmsnorm(xf, norm_final_g, F32)
    return out.reshape(batch, seq, d)
```

```python
import functools
import math
from typing import NamedTuple, Optional

import jax
import jax.numpy as jnp
from jax import lax
from jax.experimental import pallas as pl
from jax.experimental.pallas import tpu as pltpu

HEAD_DIM = 128
N_HEADS = 16
GROUP_WIDTH = N_HEADS * HEAD_DIM
QKV_WIDTH = 6 * GROUP_WIDTH
EPS = 1e-6
FFN_RESIDUAL_SCALE = 0.5
ATTN_SCALE = HEAD_DIM ** -0.5
LOG2E = math.log2(math.e)
NEG_BIG = -1e30
F32_EXP2_UNDERFLOW = -152.0
FOX_SKIP_BELOW = -160.0
NORM_BOUND_MARGIN = 1.01
FOX_DIAGONAL_BANDS = 4

LANES = 128
BF16_ROW_TILE = 16
VMEM_LIMIT_BYTES = 58 * 1024 * 1024

F32 = jnp.float32
BF16 = jnp.bfloat16


def _params(n_axes):
    return pltpu.CompilerParams(
        dimension_semantics=("arbitrary",) * n_axes,
        vmem_limit_bytes=VMEM_LIMIT_BYTES,
    )


def _dot_nt(a, b):
    return lax.dot_general(a, b, (((1,), (1,)), ((), ())), preferred_element_type=F32)


def _neg_abs(x):
    bits = lax.bitcast_convert_type(x, jnp.uint32) | jnp.uint32(0x80000000)
    return lax.bitcast_convert_type(bits, F32)


def _rmsnorm_kernel(x_ref, g_ref, o_ref):
    x = x_ref[...]
    y = x * lax.rsqrt(jnp.mean(x * x, axis=-1, keepdims=True) + EPS)
    o_ref[...] = (y * g_ref[...]).astype(o_ref.dtype)


def _rmsnorm(x, g, out_dtype, tm=512):
    m, d = x.shape
    return pl.pallas_call(
        _rmsnorm_kernel,
        grid=(m // tm,),
        in_specs=[pl.BlockSpec((tm, d), lambda i: (i, 0)),
                  pl.BlockSpec((1, d), lambda i: (0, 0))],
        out_specs=pl.BlockSpec((tm, d), lambda i: (i, 0)),
        out_shape=jax.ShapeDtypeStruct((m, d), out_dtype),
        compiler_params=_params(1),
        name="rmsnorm",
    )(x, g.reshape(1, d))


class Normed(NamedTuple):
    values: jax.Array
    sumsq: Optional[jax.Array]


def _emit_normed(y, gain_ref, values_ref, sumsq_ref, col_tile):
    values_ref[...] = (y * gain_ref[...]).astype(BF16)
    part = jnp.broadcast_to(jnp.sum(y * y, axis=-1, keepdims=True), sumsq_ref.shape)

    @pl.when(col_tile == 0)
    def _():
        sumsq_ref[...] = part

    @pl.when(col_tile > 0)
    def _():
        sumsq_ref[...] += part


def _rstd(sumsq, d):
    return lax.rsqrt(sumsq[:, :1] * (1.0 / d) + EPS)


def _normed_specs(tm, d, row_index):
    return [pl.BlockSpec((tm, d), lambda *g: (row_index(*g), 0)),
            pl.BlockSpec((tm, LANES), lambda *g: (row_index(*g), 0))]


def _gateup_kernel(*refs, tf, tail, deferred):
    if deferred:
        h_ref, ss_ref, wg_top, wg_bot, wu_top, wu_bot, o_ref = refs
    else:
        h_ref, wg_top, wg_bot, wu_top, wu_bot, o_ref = refs
    j = pl.program_id(0)
    nj = pl.num_programs(0)
    half = wg_top.shape[0]

    def step(width):
        h_top = h_ref[:, :half]
        h_bot = h_ref[:, half:]

        def project(top, bot):
            return (jnp.dot(h_top, top[:, :width].astype(BF16), preferred_element_type=F32)
                    + jnp.dot(h_bot, bot[:, :width].astype(BF16), preferred_element_type=F32))

        gate = project(wg_top, wg_bot)
        up = project(wu_top, wu_bot)
        if deferred:
            rstd = _rstd(ss_ref[...], h_ref.shape[1])
            gate = gate * rstd
            up = up * rstd
        o_ref[:, :width] = (gate * (1.0 / (1.0 + jnp.exp(-gate))) * up).astype(o_ref.dtype)

    if tail == tf:
        step(tf)
    else:
        pl.when(j < nj - 1)(lambda: step(tf))
        pl.when(j == nj - 1)(lambda: step(tail))


def _gateup(h, w_gate, w_up, tm=1024, tf=512):
    m, d = h.values.shape
    f = w_gate.shape[1]
    nj = pl.cdiv(f, tf)
    tail = f - (nj - 1) * tf
    deferred = h.sumsq is not None
    h_specs = _normed_specs(tm, d, lambda j, i: i)[:2 if deferred else 1]
    w_specs = [pl.BlockSpec((d // 2, tf), lambda j, i: (0, j)),
               pl.BlockSpec((d // 2, tf), lambda j, i: (1, j), pipeline_mode=pl.Buffered(1))]
    return pl.pallas_call(
        functools.partial(_gateup_kernel, tf=tf, tail=tail, deferred=deferred),
        grid=(nj, m // tm),
        in_specs=[*h_specs, *w_specs, *w_specs],
        out_specs=pl.BlockSpec((tm, tf), lambda j, i: (i, j)),
        out_shape=jax.ShapeDtypeStruct((m, f), BF16),
        compiler_params=_params(2),
        name="ffn_gateup",
    )(*(h if deferred else h[:1]), w_gate, w_gate, w_up, w_up)


def _down_kernel(*refs, nk, tail, next_norm):
    if next_norm:
        a_ref, w_ref, x_ref, gain_ref, o_ref, values_ref, sumsq_ref, acc_ref = refs
    else:
        a_ref, w_ref, x_ref, o_ref, acc_ref = refs
    k = pl.program_id(2)

    @pl.when(k == 0)
    def _():
        acc_ref[...] = jnp.zeros_like(acc_ref)

    @pl.when(k < nk - 1)
    def _():
        acc_ref[...] += jnp.dot(a_ref[...], w_ref[...], preferred_element_type=F32)

    @pl.when(k == nk - 1)
    def _():
        last = jnp.dot(a_ref[:, :tail], w_ref[:tail, :], preferred_element_type=F32)
        y = x_ref[...] + FFN_RESIDUAL_SCALE * (acc_ref[...] + last)
        o_ref[...] = y
        if next_norm:
            _emit_normed(y, gain_ref, values_ref, sumsq_ref, pl.program_id(1))


def _down(a, w_d, x, next_gain=None, tm=1024, tn=1024, tk=2816):
    m, f = a.shape
    n = w_d.shape[1]
    nk = pl.cdiv(f, tk)
    tail = f - (nk - 1) * tk
    next_norm = next_gain is not None
    tile = pl.BlockSpec((tm, tn), lambda i, j, k: (i, j))
    in_specs = [pl.BlockSpec((tm, tk), lambda i, j, k: (i, k)),
                pl.BlockSpec((tk, tn), lambda i, j, k: (k, j)), tile]
    out_specs, out_shape, operands = tile, jax.ShapeDtypeStruct((m, n), F32), [a, w_d, x]
    if next_norm:
        in_specs.append(pl.BlockSpec((1, tn), lambda i, j, k: (0, j)))
        operands.append(next_gain.reshape(1, n))
        out_specs = [tile, tile, pl.BlockSpec((tm, LANES), lambda i, j, k: (i, 0))]
        out_shape = [out_shape, jax.ShapeDtypeStruct((m, n), BF16), jax.ShapeDtypeStruct((m, LANES), F32)]
    out = pl.pallas_call(
        functools.partial(_down_kernel, nk=nk, tail=tail, next_norm=next_norm),
        grid=(m // tm, n // tn, nk),
        in_specs=in_specs,
        out_specs=out_specs,
        out_shape=out_shape,
        scratch_shapes=[pltpu.VMEM((tm, tn), F32)],
        compiler_params=_params(3),
        name="ffn_down",
    )(*operands)
    return (out[0], Normed(out[1], out[2])) if next_norm else out


def _ffn(x, h, w_gate, w_up, w_down_bf, next_gain=None):
    a = _gateup(h, w_gate, w_up)
    return _down(a, w_down_bf, x, next_gain)


def _inproj_kernel(h_ref, ss_ref, wt_left, wt_right, o_ref, *, heads_per_block):
    half = wt_left.shape[1]
    r = (_dot_nt(h_ref[:, :half], wt_left[...].astype(BF16))
         + _dot_nt(h_ref[:, half:], wt_right[...].astype(BF16))) * _rstd(ss_ref[...], h_ref.shape[1])
    for c in range(heads_per_block):
        o_ref[c] = r[:, c * HEAD_DIM:(c + 1) * HEAD_DIM].astype(o_ref.dtype)


def _inproj(h, w_in_t, tm=1024, tn=1024):
    m, d = h.values.shape
    hpb = tn // HEAD_DIM
    return pl.pallas_call(
        functools.partial(_inproj_kernel, heads_per_block=hpb),
        grid=(QKV_WIDTH // tn, m // tm),
        in_specs=[*_normed_specs(tm, d, lambda j, i: i),
                  pl.BlockSpec((tn, d // 2), lambda j, i: (j, 0)),
                  pl.BlockSpec((tn, d // 2), lambda j, i: (j, 1), pipeline_mode=pl.Buffered(1))],
        out_specs=pl.BlockSpec((hpb, tm, HEAD_DIM), lambda j, i: (j, i, 0)),
        out_shape=jax.ShapeDtypeStruct((QKV_WIDTH // HEAD_DIM, m, HEAD_DIM), BF16),
        compiler_params=_params(2),
        name="in_proj",
    )(*h, w_in_t, w_in_t)


def _split3(x):
    hi = x.astype(BF16)
    r = x - hi.astype(F32)
    mid = r.astype(BF16)
    lo = (r - mid.astype(F32)).astype(BF16)
    return hi, mid, lo


def _forget_kernel(h_ref, ss_ref, wt_ref, b_ref, o_ref, wt_bf, carry_ref, *, ts, n_valid):
    s = pl.program_id(1)
    valid = lax.broadcasted_iota(jnp.int32, (LANES, 1), 0) < n_valid

    @pl.when(jnp.logical_and(pl.program_id(0) == 0, s == 0))
    def _():
        wt_bf[...] = jnp.where(valid, wt_ref[...], 0.0).astype(BF16)

    @pl.when(s == 0)
    def _():
        carry_ref[...] = jnp.zeros_like(carry_ref)

    rstd_row = lax.rsqrt(ss_ref[...].T[:1, :] * (1.0 / h_ref.shape[1]) + EPS)
    logit = _dot_nt(wt_bf[...], h_ref[...]) * rstd_row + b_ref[...]
    log_f = jnp.minimum(logit, 0.0) - jnp.log1p(jnp.exp(-jnp.abs(logit)))
    r = lax.broadcasted_iota(jnp.int32, (ts, ts), 0)
    c = lax.broadcasted_iota(jnp.int32, (ts, ts), 1)
    tri = (r <= c).astype(BF16)
    hi, mid, lo = _split3(log_f)
    cum = (jnp.dot(hi, tri, preferred_element_type=F32)
           + jnp.dot(mid, tri, preferred_element_type=F32)
           + jnp.dot(lo, tri, preferred_element_type=F32)) + carry_ref[...]
    o_ref[...] = cum
    carry_ref[...] = cum[:, ts - 1:ts]


def _forget_cumsum(h, w_in_t, b_col, batch, seq, ts=512):
    d = h.values.shape[1]
    ns = seq // ts
    n_valid = w_in_t.shape[0] - QKV_WIDTH
    return pl.pallas_call(
        functools.partial(_forget_kernel, ts=ts, n_valid=n_valid),
        grid=(batch, ns),
        in_specs=[*_normed_specs(ts, d, lambda b, s: b * ns + s),
                  pl.BlockSpec((LANES, d), lambda b, s: (QKV_WIDTH // LANES, 0)),
                  pl.BlockSpec((LANES, 1), lambda b, s: (0, 0))],
        out_specs=pl.BlockSpec((None, LANES, ts), lambda b, s: (b, 0, s)),
        out_shape=jax.ShapeDtypeStruct((batch, LANES, seq), F32),
        scratch_shapes=[pltpu.VMEM((LANES, d), BF16), pltpu.VMEM((LANES, 1), F32)],
        compiler_params=_params(2),
        name="forget_cumsum",
    )(*h, w_in_t, b_col)


def _head_rmsnorm(o, gain):
    return o * lax.rsqrt(jnp.mean(o * o, axis=-1, keepdims=True) + EPS) * gain


def _fox_kernel(q_ref, k_ref, v_ref, qall_ref, c_ref, g_ref, o_ref, vaug_ref, first_ref, s_ref, m_ref, acc_ref,
                *, t, tk):
    i = pl.program_id(2)
    seq = c_ref.shape[1]

    def last_block(qi):
        return ((qi + 1) * t + tk - 1) // tk - 1

    @pl.when(i == 0)
    def _():
        vaug_ref[:, :HEAD_DIM] = v_ref[...]
        vaug_ref[:, HEAD_DIM:] = jnp.ones(v_ref.shape, BF16)
        def max_sq_norm(x_ref):
            xf = x_ref[...].astype(F32)
            return jnp.max(jnp.sum(xf * xf, axis=-1, keepdims=True), axis=0, keepdims=True)

        spread = 2.0 * NORM_BOUND_MARGIN * ATTN_SCALE * jnp.sqrt(max_sq_norm(qall_ref) * max_sq_norm(k_ref))
        c_all = c_ref[...]
        for qi in range(seq // t):
            dead = ((c_ref[:, qi * t:qi * t + 1] - c_all) + spread) * LOG2E < FOX_SKIP_BELOW
            n_dead = jnp.sum(dead.astype(F32)).astype(jnp.int32)
            first_ref[qi] = jnp.minimum(n_dead // tk, last_block(qi))

    q = q_ref[...]
    c_q0 = c_ref[:, pl.ds(pl.multiple_of(i * t, t), LANES)][:, 0:1]
    first = first_ref[i]
    last = last_block(i)

    def scores(j):
        start = pl.multiple_of(j * tk, tk)
        bias = (c_q0 - c_ref[:, pl.ds(start, tk)]) * LOG2E
        return _dot_nt(q, k_ref[pl.ds(start, tk), :]) * (ATTN_SCALE * LOG2E) + bias

    def step(j, cur, width):
        def update(r0, r1, cols, masked):
            s = s_ref[cur, r0:r1, :cols]
            if masked:
                row = i * t + r0 + lax.broadcasted_iota(jnp.int32, (r1 - r0, cols), 0)
                col = j * tk + lax.broadcasted_iota(jnp.int32, (r1 - r0, cols), 1)
                s = jnp.where(col <= row, s, NEG_BIG)
            m_old = m_ref[r0:r1]
            m_new = jnp.maximum(m_old, jnp.max(s, axis=-1, keepdims=True))
            p = jnp.exp2(s - m_new).astype(BF16)
            vs = vaug_ref[pl.ds(pl.multiple_of(j * tk, tk), cols), :]
            acc_ref[r0:r1] = (jnp.exp2(m_old - m_new) * acc_ref[r0:r1]
                              + jnp.dot(p, vs, preferred_element_type=F32))
            m_ref[r0:r1] = m_new

        if width is None:
            s_ref[1 - cur] = scores(j + 1)
            update(0, t, tk, False)
        else:
            band = t // FOX_DIAGONAL_BANDS
            for r0 in range(0, t, band):
                update(r0, r0 + band, width - t + r0 + band, True)

    def either_half(j, width):
        pl.when((j - first) % 2 == 0)(lambda: step(j, 0, width))
        pl.when((j - first) % 2 == 1)(lambda: step(j, 1, width))

    m_ref[...] = jnp.full(m_ref.shape, NEG_BIG, F32)
    acc_ref[...] = jnp.zeros(acc_ref.shape, F32)
    s_ref[0] = scores(first)

    @pl.loop(first, last)
    def _(j):
        either_half(j, None)

    reach = (i * t) % tk + t
    for width in range(t, tk + 1, t):
        pl.when(reach == width)(lambda width=width: either_half(last, width))
    acc = acc_ref[...]
    o_ref[...] = _head_rmsnorm(acc[:, :HEAD_DIM] / acc[:, HEAD_DIM:], g_ref[...]).astype(o_ref.dtype)


def _sb_kernel(q_ref, k_ref, v_ref, *refs, t, tk, heads, n_side):
    side_in, (g_ref, o_ref), side_out = refs[:n_side], refs[n_side:n_side + 2], refs[n_side + 2:2 * n_side + 2]
    rem_ref, acc_ref = refs[2 * n_side + 2:]
    for src, dst in zip(side_in, side_out):
        dst[...] = src[...].astype(BF16)
    i = pl.program_id(2)
    tri = (lax.broadcasted_iota(jnp.int32, (tk, tk), 0)
           > lax.broadcasted_iota(jnp.int32, (tk, tk), 1)).astype(BF16)
    tri2 = jnp.concatenate([tri, tri], axis=0)

    def scores(h, j, r0, r1, keep_fn):
        z = _dot_nt(q_ref[h, r0:r1, :], k_ref[h, pl.ds(pl.multiple_of(j * tk, tk), tk), :]) * (ATTN_SCALE * LOG2E)
        if keep_fn is not None:
            z = jnp.where(keep_fn(j, r0, r1), z, NEG_BIG)
        ls_pos = jnp.minimum(z, 0.0) - jnp.log2(1.0 + jnp.exp2(_neg_abs(z)))
        log_keep = ls_pos - z
        return ls_pos, log_keep, jnp.sum(log_keep, axis=-1, keepdims=True)

    def suffix_sums(log_keep):
        hi = log_keep.astype(BF16)
        lo = (log_keep - hi.astype(F32)).astype(BF16)
        return jnp.dot(jnp.concatenate([hi, lo], axis=1), tri2, preferred_element_type=F32)

    def weighted_values(h, j, ls_pos, after, rem):
        a = jnp.exp2(ls_pos + after + rem)
        return jnp.dot(a.astype(BF16), v_ref[h, pl.ds(pl.multiple_of(j * tk, tk), tk), :],
                       preferred_element_type=F32)

    def block(h, j, r0, rem):
        ls_pos, log_keep, tot = scores(h, j, r0, t, None)
        return weighted_values(h, j, ls_pos, suffix_sums(log_keep), rem), tot

    def causal(j, r0, r1):
        return (j * tk + lax.broadcasted_iota(jnp.int32, (r1 - r0, tk), 1)
                < i * t + r0 + lax.broadcasted_iota(jnp.int32, (r1 - r0, tk), 0))

    def pad_rows(x, r0, r1):
        parts = [x]
        if r0:
            parts.insert(0, jnp.zeros((r0, x.shape[1]), F32))
        if r1 < t:
            parts.append(jnp.zeros((t - r1, x.shape[1]), F32))
        return jnp.concatenate(parts, axis=0) if len(parts) > 1 else x

    n_diag = t // tk
    first = (i + 1) * n_diag - 1
    extra = first - n_diag
    group = []
    for h in range(heads):
        for d in range(n_diag):
            group.append((h, first - d, t - (d + 1) * tk, t, causal))
        group.append((h, jnp.maximum(extra, 0), 0, tk, lambda j, r0, r1: extra >= 0))
    scored = [scores(*blk) for blk in group]
    afters = [suffix_sums(log_keep) for _, log_keep, _ in scored]
    rem = [jnp.zeros((t, 1), F32) for _ in range(heads)]
    rest_alive = [False] * heads
    outs = []
    for (h, j, r0, r1, _), (ls_pos, _, tot), after in zip(group, scored, afters):
        if r1 < t:
            rest_alive[h] = jnp.max(rem[h][tk:]) > F32_EXP2_UNDERFLOW
        outs.append(weighted_values(h, j, ls_pos, after, rem[h][r0:r1]))
        rem[h] = rem[h] + pad_rows(tot, r0, r1)
    for h in range(heads):
        acc_ref[h] = sum(pad_rows(out, r0, r1) for (hh, _, r0, r1, _), out in zip(group, outs) if hh == h)
        rem_ref[h] = rem[h]
    rem_max = [jnp.max(rem[h]) for h in range(heads)]

    def cond(state):
        j, rem_max = state
        return jnp.logical_and(j >= 0, rem_max > F32_EXP2_UNDERFLOW)

    for h in range(heads):
        if tk < t:
            @pl.when(jnp.logical_and(rest_alive[h], extra >= 0))
            def _(h=h):
                out, tot = block(h, extra, tk, rem_ref[h, tk:])
                acc_ref[h, tk:] += out
                rem_ref[h, tk:] += tot

        def body(state, h=h):
            j, _ = state
            rem = rem_ref[h]
            out, tot = block(h, j, 0, rem)
            acc_ref[h] += out
            rem_ref[h] = rem + tot
            return j - 1, jnp.max(rem + tot)

        lax.while_loop(cond, body, (extra - 1, rem_max[h]))

    for h in range(heads):
        o_ref[:, h * HEAD_DIM:(h + 1) * HEAD_DIM] = _head_rmsnorm(acc_ref[h], g_ref[h]).astype(o_ref.dtype)


def _attention_call(kernel_fn, name, qkvh, first_section, extra_inputs, extra_specs, gain, batch, seq, t,
                    scratch_shapes=(), heads=None, side_casts=()):
    nq = seq // t
    per_step = heads or 1
    groups = N_HEADS // per_step

    def section_spec(section, rows):
        blocks_per_batch = seq // rows
        return pl.BlockSpec(
            (heads, rows, HEAD_DIM),
            lambda b, g, i: (section * groups + g, b * blocks_per_batch + (i if rows == t else 0), 0))

    side_specs = []
    for w in side_casts:
        slab = -(-pl.cdiv(w.shape[0], batch * groups * nq) // BF16_ROW_TILE) * BF16_ROW_TILE
        n_slabs = pl.cdiv(w.shape[0], slab)
        side_specs.append(pl.BlockSpec(
            (slab, w.shape[1]),
            lambda b, g, i, n_slabs=n_slabs: (jnp.minimum((b * groups + g) * nq + i, n_slabs - 1), 0)))

    out = pl.pallas_call(
        kernel_fn,
        grid=(batch, groups, nq),
        in_specs=[section_spec(first_section, t), section_spec(first_section + 1, seq),
                  section_spec(first_section + 2, seq), *extra_specs, *side_specs,
                  pl.BlockSpec((heads, 1, HEAD_DIM), lambda b, g, i: (g, 0, 0))],
        out_specs=[pl.BlockSpec((t, per_step * HEAD_DIM), lambda b, g, i: (b * nq + i, g)), *side_specs],
        out_shape=[jax.ShapeDtypeStruct((batch * seq, GROUP_WIDTH), BF16),
                   *(jax.ShapeDtypeStruct(w.shape, BF16) for w in side_casts)],
        scratch_shapes=list(scratch_shapes),
        compiler_params=_params(3),
        name=name,
    )(qkvh, qkvh, qkvh, *extra_inputs, *side_casts, gain.reshape(N_HEADS, 1, HEAD_DIM))
    return tuple(out)


def _attention(qkvh, c_rows, sb_gain, fox_gain, batch, seq, later_weights=(), t_sb=512, tk_sb=256, heads_sb=4,
               t_fox=1024, tk_fox=1024):
    assert t_sb % tk_sb == 0 and seq % t_sb == 0 and N_HEADS % heads_sb == 0
    sb_scratch = [pltpu.VMEM((heads_sb, t_sb, 1), F32), pltpu.VMEM((heads_sb, t_sb, HEAD_DIM), F32)]
    sb_kernel = functools.partial(_sb_kernel, t=t_sb, tk=tk_sb, heads=heads_sb, n_side=len(later_weights))
    o_sb, *weights_bf = _attention_call(sb_kernel, "sb_attention", qkvh, 0, (), (), sb_gain, batch, seq, t_sb,
                                        sb_scratch, heads=heads_sb, side_casts=tuple(later_weights))
    assert seq % tk_fox == 0 and tk_fox % t_fox == 0
    fox_q_section = 3
    q_all_spec = pl.BlockSpec((None, seq, HEAD_DIM), lambda b, h, i: (fox_q_section * N_HEADS + h, b, 0))
    c_spec = pl.BlockSpec((None, None, 1, seq), lambda b, h, i: (b, h, 0, 0))
    fox_scratch = [pltpu.VMEM((seq, 2 * HEAD_DIM), BF16),
                   pltpu.SMEM((seq // t_fox,), jnp.int32),
                   pltpu.VMEM((2, t_fox, tk_fox), F32),
                   pltpu.VMEM((t_fox, 1), F32),
                   pltpu.VMEM((t_fox, 2 * HEAD_DIM), F32)]
    (o_fox,) = _attention_call(functools.partial(_fox_kernel, t=t_fox, tk=tk_fox), "fox_attention", qkvh,
                               fox_q_section, (qkvh, c_rows), (q_all_spec, c_spec), fox_gain, batch, seq,
                               t_fox, fox_scratch)
    return o_sb, o_fox, weights_bf


def _outproj_kernel(a_ref, b_ref, w_ref, x_ref, gain_ref, o_ref, values_ref, sumsq_ref):
    acc = jnp.dot(a_ref[...], w_ref[:GROUP_WIDTH, :], preferred_element_type=F32)
    acc = acc + jnp.dot(b_ref[...], w_ref[GROUP_WIDTH:, :], preferred_element_type=F32)
    y = x_ref[...] + acc
    o_ref[...] = y
    _emit_normed(y, gain_ref, values_ref, sumsq_ref, pl.program_id(1))


def _outproj(a, b, w_o, x, next_gain, tm=1024, tn=512):
    m = a.shape[0]
    n = w_o.shape[1]
    tile = pl.BlockSpec((tm, tn), lambda i, j: (i, j))
    y, values, sumsq = pl.pallas_call(
        _outproj_kernel,
        grid=(m // tm, n // tn),
        in_specs=[pl.BlockSpec((tm, GROUP_WIDTH), lambda i, j: (i, 0)),
                  pl.BlockSpec((tm, GROUP_WIDTH), lambda i, j: (i, 0)),
                  pl.BlockSpec((2 * GROUP_WIDTH, tn), lambda i, j: (0, j)),
                  tile,
                  pl.BlockSpec((1, tn), lambda i, j: (0, j))],
        out_specs=[tile, tile, pl.BlockSpec((tm, LANES), lambda i, j: (i, 0))],
        out_shape=[jax.ShapeDtypeStruct((m, n), F32), jax.ShapeDtypeStruct((m, n), BF16),
                   jax.ShapeDtypeStruct((m, LANES), F32)],
        compiler_params=_params(2),
        name="out_proj",
    )(a, b, w_o, x, next_gain.reshape(1, n))
    return y, Normed(values, sumsq)


def _mixer(x, h, w_in, b_f, sb_g, fox_g, w_o, next_gain, next_w_down, batch, seq):
    w_in_t = w_in.T
    qkvh = _inproj(h, w_in_t)
    b_col = jnp.pad(b_f, (0, LANES - N_HEADS)).reshape(LANES, 1)
    c = _forget_cumsum(h, w_in_t, b_col, batch, seq)
    c_rows = c[:, :N_HEADS, :].reshape(batch, N_HEADS, 1, seq)
    o_sb, o_fox, (w_o_bf, next_w_down_bf) = _attention(qkvh, c_rows, sb_g, fox_g, batch, seq,
                                                       later_weights=(w_o, next_w_down))
    return (*_outproj(o_sb, o_fox, w_o_bf, x, next_gain), next_w_down_bf)


@jax.jit
def kernel(x, norm_ffn1_g, ffn1_w_gate, ffn1_w_up, ffn1_w_down, norm_mix_g, w_in, b_f,
           sb_out_g, fox_out_g, w_o, norm_ffn2_g, ffn2_w_gate, ffn2_w_up, ffn2_w_down,
           norm_final_g):
    batch, seq, d = x.shape
    depth = w_in.shape[0]
    xf = x.reshape(batch * seq, d)
    h = Normed(_rmsnorm(xf, norm_ffn1_g[0], BF16), None)
    for l in range(depth):
        xf, h = _ffn(xf, h, ffn1_w_gate[l], ffn1_w_up[l], ffn1_w_down[l].astype(BF16), next_gain=norm_mix_g[l])
        xf, h, ffn2_w_down_bf = _mixer(xf, h, w_in[l], b_f[l], sb_out_g[l], fox_out_g[l], w_o[l],
                                       norm_ffn2_g[l], ffn2_w_down[l], batch, seq)
        if l + 1 < depth:
            xf, h = _ffn(xf, h, ffn2_w_gate[l], ffn2_w_up[l], ffn2_w_down_bf, next_gain=norm_ffn1_g[l + 1])
        else:
            xf = _ffn(xf, h, ffn2_w_gate[l], ffn2_w_up[l], ffn2_w_down_bf)
    out = _rmsnorm(xf, norm_final_g, F32)
    return out.reshape(batch, seq, d)
```
